```python
import jax
import jax.numpy as jnp
from jax import lax
import numpy as np

D_MODEL = 2048
BATCH = 4
SEQ = 2048
DEPTH = 1

MLSTM_HEADS = 8
MLSTM_DV = D_MODEL // 16
MLSTM_DK = MLSTM_DV // 2
MLSTM_CHUNK = 64
GATE_SOFTCAP = 15.0
MLSTM_WIDTH = MLSTM_HEADS * MLSTM_DV

MOBA_HEADS = 8
MOBA_HEAD_DIM = D_MODEL // 16
MOBA_BLOCK = 256
MOBA_TOPK = 3
MOBA_QCHUNK = 64
MOBA_WIDTH = MOBA_HEADS * MOBA_HEAD_DIM

N_EXPERTS = 32
TOP_K = 4
D_EXPERT = D_MODEL
SWIGLU_ALPHA = 1.702
SWIGLU_LIMIT = 7.0

NORM_EPS = 1e-6
NEG_INF = -1e30

IN_WIDTHS = (MLSTM_HEADS * MLSTM_DK, MLSTM_HEADS * MLSTM_DK, MLSTM_WIDTH, MLSTM_WIDTH,
             MLSTM_HEADS, MLSTM_HEADS, MOBA_WIDTH, MOBA_WIDTH, MOBA_WIDTH, D_MODEL, D_MODEL)
D_IN = sum(IN_WIDTHS)

kernel_name = 'hybrid_mlstm_moba_moe_block'


def rms_norm(x, g):
    xf = x.astype(jnp.float32)
    y = xf * lax.rsqrt(jnp.mean(xf * xf, axis=-1, keepdims=True) + NORM_EPS)
    return (y * g).astype(x.dtype)


def modulate(h, shift, scale):
    return h * (1.0 + scale[:, None, :]) + shift[:, None, :]


def soft_cap(x, cap):
    return cap * jnp.tanh(x / cap)


def mlstm_chunkwise(q, k, v, log_i, log_f):
    B, H, S, DK = q.shape
    DV = v.shape[-1]
    L = MLSTM_CHUNK
    NC = S // L
    q = q.reshape(B, H, NC, L, DK)
    k = k.reshape(B, H, NC, L, DK)
    v = v.reshape(B, H, NC, L, DV)
    li = log_i.reshape(B, H, NC, L)
    lf = log_f.reshape(B, H, NC, L)
    b = jnp.cumsum(lf, axis=-1)
    g = b[..., -1]
    a = g[..., None] - b + li
    m_loc = jnp.max(a, axis=-1)
    w = jnp.exp(a - m_loc[..., None])
    C_loc = jnp.einsum('bhcl,bhcld,bhcle->bhcde', w, v, k)
    n_loc = jnp.einsum('bhcl,bhcle->bhce', w, k)

    def step(carry, inp):
        C, n, m = carry
        Cl, nl, ml, gc = inp
        m_new = jnp.maximum(gc + m, ml)
        s_prev = jnp.exp(gc + m - m_new)
        s_loc = jnp.exp(ml - m_new)
        C_new = s_prev[..., None, None] * C + s_loc[..., None, None] * Cl
        n_new = s_prev[..., None] * n + s_loc[..., None] * nl
        return (C_new, n_new, m_new), (C, n, m)

    def to_front(t):
        return jnp.moveaxis(t, 2, 0)

    init = (jnp.zeros((B, H, DV, DK), jnp.float32), jnp.zeros((B, H, DK), jnp.float32),
            jnp.zeros((B, H), jnp.float32))
    _, (C_in, n_in, m_in) = lax.scan(step, init, (to_front(C_loc), to_front(n_loc),
                                                   to_front(m_loc), to_front(g)))
    C_in = jnp.moveaxis(C_in, 0, 2)
    n_in = jnp.moveaxis(n_in, 0, 2)
    m_in = jnp.moveaxis(m_in, 0, 2)
    causal = jnp.tril(jnp.ones((L, L), dtype=bool))
    d = jnp.where(causal, b[..., :, None] - b[..., None, :] + li[..., None, :], -jnp.inf)
    inter_log = b + m_in[..., None]
    m_j = jnp.maximum(inter_log, jnp.max(d, axis=-1))
    s_intra = jnp.einsum('bhcjd,bhcsd->bhcjs', q, k) * jnp.exp(d - m_j[..., None])
    s_inter = jnp.exp(inter_log - m_j)
    num = (jnp.einsum('bhcjs,bhcsd->bhcjd', s_intra, v)
           + s_inter[..., None] * jnp.einsum('bhcde,bhcje->bhcjd', C_in, q))
    den = jnp.sum(s_intra, axis=-1) + s_inter * jnp.einsum('bhce,bhcje->bhcj', n_in, q)
    h = num / jnp.maximum(jnp.abs(den), jnp.exp(-m_j))[..., None]
    return h.reshape(B, H, S, DV)


def moba_attention(q, k, v):
    B, H, S, Dh = q.shape
    nb = -(-S // MOBA_BLOCK)
    pad = nb * MOBA_BLOCK - S
    kp = jnp.pad(k, ((0, 0), (0, 0), (0, pad), (0, 0)))
    vp = jnp.pad(v, ((0, 0), (0, 0), (0, pad), (0, 0)))
    kb = kp.reshape(B, H, nb, MOBA_BLOCK, Dh)
    vb = vp.reshape(B, H, nb, MOBA_BLOCK, Dh)
    k_mean = jnp.mean(kb.astype(jnp.float32), axis=3)
    q_blk = jnp.arange(S) // MOBA_BLOCK
    past = jnp.arange(nb)[None, :] < q_blk[:, None]
    gate = jnp.einsum('bhsd,bhnd->bhsn', q.astype(jnp.float32), k_mean)
    gate = jnp.where(past, gate, NEG_INF)
    n_sel = min(MOBA_TOPK, nb)
    _, sel = lax.top_k(gate, n_sel)
    sel_ok = sel < q_blk[:, None]
    nq = S // MOBA_QCHUNK

    def to_chunks(t):
        return jnp.moveaxis(t.reshape(B, H, nq, MOBA_QCHUNK, *t.shape[3:]), 2, 0)

    bi = jnp.arange(B)[:, None, None]
    hi = jnp.arange(H)[None, :, None]
    scale = Dh ** -0.5

    def attend_chunk(args):
        ci, qc, sc, okc = args
        q0 = ci * MOBA_QCHUNK
        blk = q0 // MOBA_BLOCK
        k_own = lax.dynamic_slice_in_dim(kp, blk * MOBA_BLOCK, MOBA_BLOCK, axis=2)
        v_own = lax.dynamic_slice_in_dim(vp, blk * MOBA_BLOCK, MOBA_BLOCK, axis=2)
        q_pos = q0 + jnp.arange(MOBA_QCHUNK)
        k_pos = blk * MOBA_BLOCK + jnp.arange(MOBA_BLOCK)
        own = jnp.einsum('bhqd,bhkd->bhqk', qc, k_own).astype(jnp.float32) * scale
        own = jnp.where(k_pos[None, :] <= q_pos[:, None], own, NEG_INF)
        logits = []
        for j in range(n_sel):
            kj = kb[bi, hi, sc[..., j]]
            lj = jnp.einsum('bhqd,bhqkd->bhqk', qc, kj).astype(jnp.float32) * scale
            logits.append(jnp.where(okc[..., j, None], lj, NEG_INF))
        logits.append(own)
        p = jax.nn.softmax(jnp.concatenate(logits, axis=-1), axis=-1)
        out = jnp.einsum('bhqk,bhkd->bhqd', p[..., n_sel * MOBA_BLOCK:], v_own.astype(jnp.float32))
        for j in range(n_sel):
            vj = vb[bi, hi, sc[..., j]]
            out = out + jnp.einsum('bhqk,bhqkd->bhqd',
                                   p[..., j * MOBA_BLOCK:(j + 1) * MOBA_BLOCK],
                                   vj.astype(jnp.float32))
        return out.astype(q.dtype)

    out = lax.map(attend_chunk, (jnp.arange(nq), to_chunks(q), to_chunks(sel), to_chunks(sel_ok)))
    return jnp.moveaxis(out, 0, 2).reshape(B, H, S, Dh)


def expert_ffn(t, w_up, b_up, w_down, b_down):
    hh = t @ w_up + b_up
    glu, lin = jnp.split(hh, 2, axis=-1)
    glu = jnp.minimum(glu, SWIGLU_LIMIT)
    lin = jnp.clip(lin, -SWIGLU_LIMIT, SWIGLU_LIMIT)
    act = glu * jax.nn.sigmoid(SWIGLU_ALPHA * glu) * (lin + 1.0)
    return act @ w_down + b_down


def setup_inputs(seed: int = 0) -> dict:
    key = jax.random.key(seed)
    ks = jax.random.split(key, 24)
    L = DEPTH
    D = D_MODEL

    def nrm(k, shape, scale):
        return jax.random.normal(k, shape, jnp.float32) * scale

    return {
        'x': nrm(ks[0], (BATCH, SEQ, D), 1.0),
        'c': nrm(ks[1], (BATCH, D), 1.0),
        'w_ada': nrm(ks[2], (L, D, 6 * D), 0.5 * D ** -0.5),
        'b_ada': nrm(ks[3], (L, 6 * D), 0.02),
        'g_mix': 1.0 + nrm(ks[4], (L, D), 0.02),
        'w_in': nrm(ks[5], (L, D, D_IN), D ** -0.5),
        'b_igate': nrm(ks[6], (L, MLSTM_HEADS), 0.1),
        'b_fgate': jnp.linspace(3.0, 6.0, MLSTM_HEADS, dtype=jnp.float32)[None, :]
                   + nrm(ks[7], (L, MLSTM_HEADS), 0.1),
        'g_mlstm_out': 1.0 + nrm(ks[8], (L, MLSTM_HEADS, MLSTM_DV), 0.02),
        'g_q': 1.0 + nrm(ks[9], (L, MOBA_HEAD_DIM), 0.02),
        'g_k': 1.0 + nrm(ks[10], (L, MOBA_HEAD_DIM), 0.02),
        'w_branch_a': nrm(ks[11], (L, MLSTM_WIDTH, D), MLSTM_WIDTH ** -0.5),
        'w_branch_b': nrm(ks[12], (L, MOBA_WIDTH, D), MOBA_WIDTH ** -0.5),
        'w_out': nrm(ks[13], (L, D, D), D ** -0.5),
        'g_ffn': 1.0 + nrm(ks[14], (L, D), 0.02),
        'w_router': nrm(ks[15], (L, D, N_EXPERTS), D ** -0.5),
        'b_router': nrm(ks[16], (L, N_EXPERTS), 0.01),
        'w_up': nrm(ks[17], (L, N_EXPERTS, D, 2 * D_EXPERT), D ** -0.5),
        'b_up': nrm(ks[18], (L, N_EXPERTS, 2 * D_EXPERT), 0.02),
        'w_down': nrm(ks[19], (L, N_EXPERTS, D_EXPERT, D), D_EXPERT ** -0.5),
        'b_down': nrm(ks[20], (L, N_EXPERTS, D), 0.02),
    }


def reference(x, c, w_ada, b_ada, g_mix, w_in, b_igate, b_fgate, g_mlstm_out, g_q, g_k,
              w_branch_a, w_branch_b, w_out, g_ffn, w_router, b_router, w_up, b_up,
              w_down, b_down):
    B, S, D = x.shape
    offsets = []
    acc = 0
    for wdt in IN_WIDTHS[:-1]:
        acc += wdt
        offsets.append(acc)
    for l in range(DEPTH):
        mod = jax.nn.silu(c) @ w_ada[l] + b_ada[l]
        sh1, sc1, gt1, sh2, sc2, gt2 = jnp.split(mod, 6, axis=-1)

        h = modulate(rms_norm(x, g_mix[l]), sh1, sc1)
        z = h @ w_in[l]
        qa, ka, va, oa, ia, fa, qb, kb, vb, ga, gb = jnp.split(z, offsets, axis=-1)

        q_m = qa.reshape(B, S, MLSTM_HEADS, MLSTM_DK).transpose(0, 2, 1, 3).astype(jnp.float32) * (MLSTM_DK ** -0.5)
        k_m = ka.reshape(B, S, MLSTM_HEADS, MLSTM_DK).transpose(0, 2, 1, 3).astype(jnp.float32)
        v_m = va.reshape(B, S, MLSTM_HEADS, MLSTM_DV).transpose(0, 2, 1, 3).astype(jnp.float32)
        log_i = soft_cap((ia + b_igate[l]).astype(jnp.float32), GATE_SOFTCAP).transpose(0, 2, 1)
        log_f = jax.nn.log_sigmoid(soft_cap((fa + b_fgate[l]).astype(jnp.float32), GATE_SOFTCAP)).transpose(0, 2, 1)
        h_m = mlstm_chunkwise(q_m, k_m, v_m, log_i, log_f).transpose(0, 2, 1, 3)
        h_m = rms_norm(h_m, g_mlstm_out[l]).astype(x.dtype)
        h_m = h_m * jax.nn.sigmoid(oa.reshape(B, S, MLSTM_HEADS, MLSTM_DV))
        y_a = h_m.reshape(B, S, MLSTM_WIDTH) @ w_branch_a[l]

        q_b = rms_norm(qb.reshape(B, S, MOBA_HEADS, MOBA_HEAD_DIM), g_q[l]).transpose(0, 2, 1, 3)
        k_b = rms_norm(kb.reshape(B, S, MOBA_HEADS, MOBA_HEAD_DIM), g_k[l]).transpose(0, 2, 1, 3)
        v_b = vb.reshape(B, S, MOBA_HEADS, MOBA_HEAD_DIM).transpose(0, 2, 1, 3)
        o_b = moba_attention(q_b, k_b, v_b).transpose(0, 2, 1, 3).reshape(B, S, MOBA_WIDTH)
        y_b = o_b.astype(x.dtype) @ w_branch_b[l]

        y = jax.nn.sigmoid(ga) * y_a + jax.nn.sigmoid(gb) * y_b
        x = x + gt1[:, None, :] * (y @ w_out[l])

        t = modulate(rms_norm(x, g_ffn[l]), sh2, sc2).reshape(B * S, D)
        logits = (t @ w_router[l] + b_router[l]).astype(jnp.float32)
        top_val, top_idx = lax.top_k(logits, TOP_K)
        top_w = jax.nn.softmax(top_val, axis=-1)
        combine = jnp.sum(jax.nn.one_hot(top_idx, N_EXPERTS, dtype=jnp.float32) * top_w[..., None],
                          axis=1).astype(t.dtype)
        moe = jnp.zeros_like(t)
        for e in range(N_EXPERTS):
            out_e = expert_ffn(t, w_up[l, e], b_up[l, e], w_down[l, e], b_down[l, e])
            moe = moe + combine[:, e:e + 1] * out_e.astype(t.dtype)
        x = x + gt2[:, None, :] * moe.reshape(B, S, D)
    return x
```

```python
import functools

import jax
import jax.numpy as jnp
from jax import lax
from jax.experimental import pallas as pl
from jax.experimental.pallas import tpu as pltpu

F32 = jnp.float32
BF16 = jnp.bfloat16

D_MODEL = 2048
MLSTM_HEADS = 8
MLSTM_DV = 128
MLSTM_DK = 64
MLSTM_WIDTH = MLSTM_HEADS * MLSTM_DV
GATE_SOFTCAP = 15.0
MOBA_HEADS = 8
MOBA_HEAD_DIM = 128
MOBA_BLOCK = 256
MOBA_TOPK = 3
MOBA_WIDTH = MOBA_HEADS * MOBA_HEAD_DIM
N_EXPERTS = 32
TOP_K = 4
D_EXPERT = D_MODEL
SWIGLU_ALPHA = 1.702
SWIGLU_LIMIT = 7.0
NORM_EPS = 1e-6
NEG_INF = -1e30

LANES = 128
VMEM_LIMIT = 56 * 1024 * 1024

Z_QA, Z_KA, Z_VA, Z_OA = 0, 512, 1024, 2048
Z_QB, Z_KB, Z_VB = 3072, 4096, 5120
Z_GA, Z_GB = 6144, 8192
Z_WIDTH = 10240

MLSTM_CHUNK = 256


def _params(*sem):
    return pltpu.CompilerParams(dimension_semantics=sem, vmem_limit_bytes=VMEM_LIMIT)


def _rms(x, eps=NORM_EPS):
    return x * lax.rsqrt(jnp.mean(x * x, axis=-1, keepdims=True) + eps)


def _sigmoid(x):
    return 1.0 / (1.0 + jnp.exp(-x))


def _ada_kernel(c_ref, w_ref, b_ref, o_ref):
    c = c_ref[...]
    s = (c * _sigmoid(c)).astype(BF16)
    o_ref[...] = jnp.dot(s, w_ref[...].astype(BF16), preferred_element_type=F32) + b_ref[...]


def _ada(c_pad, w_ada, b_ada):
    rows, d = c_pad.shape
    n = w_ada.shape[1]
    tn = 1024
    return pl.pallas_call(
        _ada_kernel,
        grid=(n // tn,),
        in_specs=[pl.BlockSpec((rows, d), lambda j: (0, 0)),
                  pl.BlockSpec((d, tn), lambda j: (0, j)),
                  pl.BlockSpec((1, tn), lambda j: (0, j))],
        out_specs=pl.BlockSpec((rows, tn), lambda j: (0, j)),
        out_shape=jax.ShapeDtypeStruct((rows, n), F32),
        compiler_params=_params("arbitrary"),
        name="ada",
    )(c_pad, w_ada, b_ada.reshape(1, n))


def _in_kernel(x_ref, mod_ref, g_ref, w_ref, wg_ref, z_ref, gates_ref, h_scr):
    @pl.when(pl.program_id(1) == 0)
    def _():
        y = _rms(x_ref[...]) * g_ref[...]
        h = y * (1.0 + mod_ref[1:2, :]) + mod_ref[0:1, :]
        hb = h.astype(BF16)
        h_scr[...] = hb
        gates_ref[...] = jnp.dot(hb, wg_ref[...], preferred_element_type=F32)

    z_ref[...] = jnp.dot(h_scr[...], w_ref[...], preferred_element_type=F32).astype(z_ref.dtype)


def _in_proj(x2, mod3, g_mix, w_main, w_gate, seq):
    t, d = x2.shape
    n = w_main.shape[1]
    tm, tn = 512, 1024
    per_b = seq // tm
    return pl.pallas_call(
        _in_kernel,
        grid=(t // tm, n // tn),
        in_specs=[pl.BlockSpec((tm, d), lambda i, j: (i, 0)),
                  pl.BlockSpec((None, 8, d), lambda i, j: (i // per_b, 0, 0)),
                  pl.BlockSpec((1, d), lambda i, j: (0, 0)),
                  pl.BlockSpec((d, tn), lambda i, j: (0, j)),
                  pl.BlockSpec((d, LANES), lambda i, j: (0, 0))],
        out_specs=[pl.BlockSpec((tm, tn), lambda i, j: (i, j)),
                   pl.BlockSpec((tm, LANES), lambda i, j: (i, 0))],
        out_shape=[jax.ShapeDtypeStruct((t, n), BF16),
                   jax.ShapeDtypeStruct((t, LANES), F32)],
        scratch_shapes=[pltpu.VMEM((tm, d), BF16)],
        compiler_params=_params("arbitrary", "arbitrary"),
        name="in_proj",
    )(x2, mod3, g_mix, w_main, w_gate)


def _mlstm_kernel(q_ref, k_ref, v_ref, o_ref, gates_ref, gbias_ref, gout_ref, out_ref,
                  c_scr, n_scr, m_scr):
    L = MLSTM_CHUNK
    pair = pl.program_id(1)

    @pl.when(pl.program_id(2) == 0)
    def _():
        c_scr[...] = jnp.zeros_like(c_scr)
        n_scr[...] = jnp.zeros_like(n_scr)
        m_scr[...] = jnp.zeros_like(m_scr)

    pre = gates_ref[...] + gbias_ref[...]
    capped = GATE_SOFTCAP * jnp.tanh(pre / GATE_SOFTCAP)
    log_f = jnp.minimum(capped, 0.0) - jnp.log1p(jnp.exp(-jnp.abs(capped)))
    row = lax.broadcasted_iota(jnp.int32, (L, L), 0)
    col = lax.broadcasted_iota(jnp.int32, (L, L), 1)
    causal = col <= row
    tril = jnp.where(causal, 1.0, 0.0).astype(F32)
    b_all = jnp.dot(tril, log_f, preferred_element_type=F32,
                    precision=lax.Precision.HIGHEST)
    li_t = capped.T
    b_t = b_all.T

    for j in range(2):
        head = 2 * pair + j
        q = q_ref[:, j * MLSTM_DK:(j + 1) * MLSTM_DK] * (MLSTM_DK ** -0.5)
        k = k_ref[:, j * MLSTM_DK:(j + 1) * MLSTM_DK]
        v = v_ref[:, j * MLSTM_DV:(j + 1) * MLSTM_DV]
        lane = lax.broadcasted_iota(jnp.int32, (L, LANES), 1)
        sub = lax.broadcasted_iota(jnp.int32, (LANES, L), 0)
        li_c = jnp.sum(jnp.where(lane == head, capped, 0.0), axis=1, keepdims=True)
        b_c = jnp.sum(jnp.where(lane == head + MLSTM_HEADS, b_all, 0.0), axis=1, keepdims=True)
        li_r = jnp.sum(jnp.where(sub == head, li_t, 0.0), axis=0, keepdims=True)
        b_r = jnp.sum(jnp.where(sub == head + MLSTM_HEADS, b_t, 0.0), axis=0, keepdims=True)
        ct = c_scr[j]
        nvec = n_scr[j]
        m_in = m_scr[j][:, :1]
        g = b_c[L - 1:L, :]

        d = jnp.where(causal, b_c - b_r + li_r, NEG_INF)
        dmax = jnp.max(d, axis=-1, keepdims=True)
        inter = b_c + m_in
        m_j = jnp.maximum(inter, dmax)
        p = jnp.exp(d - m_j)
        s = lax.dot_general(q, k, (((1,), (1,)), ((), ())), preferred_element_type=F32) * p
        s_inter = jnp.exp(inter - m_j)
        qf = q.astype(F32)
        num = (jnp.dot(s.astype(BF16), v, preferred_element_type=F32)
               + s_inter * jnp.dot(q, ct.astype(BF16), preferred_element_type=F32))
        den = (jnp.sum(s, axis=-1, keepdims=True)
               + s_inter * jnp.sum(qf * nvec, axis=-1, keepdims=True))
        h = num / jnp.maximum(jnp.abs(den), jnp.exp(-m_j))

        gout = gout_ref[pl.ds(head, 1), :]
        hn = _rms(h) * gout
        og = o_ref[:, j * MLSTM_DV:(j + 1) * MLSTM_DV].astype(F32)
        out_ref[:, j * MLSTM_DV:(j + 1) * MLSTM_DV] = (hn * _sigmoid(og)).astype(out_ref.dtype)

        a = g - b_c + li_c
        m_loc = jnp.max(a, axis=0, keepdims=True)
        w = jnp.exp(a - m_loc)
        kw = k.astype(F32) * w
        c_loc = lax.dot_general(kw.astype(BF16), v, (((0,), (0,)), ((), ())),
                                preferred_element_type=F32)
        n_loc = jnp.sum(kw, axis=0, keepdims=True)
        m_new = jnp.maximum(g + m_in, m_loc)
        s_prev = jnp.exp(g + m_in - m_new)
        s_loc = jnp.exp(m_loc - m_new)
        c_scr[j] = s_prev * ct + s_loc * c_loc
        n_scr[j] = s_prev * nvec + s_loc * n_loc
        m_scr[j] = jnp.broadcast_to(m_new, (1, LANES))


def _mlstm(z, gates, gbias, g_out, batch, seq):
    t = z.shape[0]
    L = MLSTM_CHUNK
    nc = seq // L
    pairs = MLSTM_HEADS // 2
    rowmap = lambda b, p, c: b * nc + c
    return pl.pallas_call(
        _mlstm_kernel,
        grid=(batch, pairs, nc),
        in_specs=[pl.BlockSpec((L, 128), lambda b, p, c: (rowmap(b, p, c), Z_QA // 128 + p)),
                  pl.BlockSpec((L, 128), lambda b, p, c: (rowmap(b, p, c), Z_KA // 128 + p)),
                  pl.BlockSpec((L, 256), lambda b, p, c: (rowmap(b, p, c), Z_VA // 256 + p)),
                  pl.BlockSpec((L, 256), lambda b, p, c: (rowmap(b, p, c), Z_OA // 256 + p)),
                  pl.BlockSpec((L, LANES), lambda b, p, c: (rowmap(b, p, c), 0)),
                  pl.BlockSpec((1, LANES), lambda b, p, c: (0, 0)),
                  pl.BlockSpec((MLSTM_HEADS, MLSTM_DV), lambda b, p, c: (0, 0))],
        out_specs=pl.BlockSpec((L, 256), lambda b, p, c: (rowmap(b, p, c), p)),
        out_shape=jax.ShapeDtypeStruct((t, MLSTM_WIDTH), BF16),
        scratch_shapes=[pltpu.VMEM((2, MLSTM_DK, MLSTM_DV), F32),
                        pltpu.VMEM((2, 1, MLSTM_DK), F32),
                        pltpu.VMEM((2, 1, LANES), F32)],
        compiler_params=_params("arbitrary", "arbitrary", "arbitrary"),
        name="mlstm",
    )(z, z, z, z, gates, gbias, g_out)


def _moba_kernel(q_ref, k_ref, v_ref, gq_ref, gk_ref, out_ref,
                 kn_scr, kmean_scr, m_scr, l_scr, acc_scr):
    BK = MOBA_BLOCK
    nb = k_ref.shape[0] // BK
    qi = pl.program_id(2)
    scale = MOBA_HEAD_DIM ** -0.5

    @pl.when(qi == 0)
    def _():
        kmean_scr[...] = jnp.zeros_like(kmean_scr)
        for n in range(nb):
            kf = k_ref[n * BK:(n + 1) * BK, :].astype(F32)
            kn = _rms(kf) * gk_ref[...]
            kn_scr[n * BK:(n + 1) * BK, :] = kn.astype(BF16)
            kmean_scr[n:n + 1, :] = jnp.mean(kn, axis=0, keepdims=True)

    qn = _rms(q_ref[...].astype(F32)) * gq_ref[...]
    qb = qn.astype(BF16)
    gate = lax.dot_general(qn, kmean_scr[...], (((1,), (1,)), ((), ())),
                           preferred_element_type=F32,
                           precision=lax.Precision.HIGHEST)
    lane = lax.broadcasted_iota(jnp.int32, (BK, LANES), 1)
    cnt = jnp.zeros((BK, LANES), F32)
    for m in range(nb):
        gm = gate[:, m:m + 1]
        beats = (gm > gate) | ((gm == gate) & (m < lane))
        cnt = cnt + jnp.where(beats & (m < qi), 1.0, 0.0)
    sel = jnp.where((cnt < MOBA_TOPK) & (lane < qi), 1.0, 0.0)

    row = lax.broadcasted_iota(jnp.int32, (BK, BK), 0)
    col = lax.broadcasted_iota(jnp.int32, (BK, BK), 1)
    own0 = pl.multiple_of(qi * BK, BK)
    k_own = kn_scr[pl.ds(own0, BK), :]
    v_own = v_ref[pl.ds(own0, BK), :]
    s = lax.dot_general(qb, k_own, (((1,), (1,)), ((), ())), preferred_element_type=F32) * scale
    s = jnp.where(col <= row, s, NEG_INF)
    m0 = jnp.max(s, axis=-1, keepdims=True)
    p = jnp.exp(s - m0)
    m_scr[...] = jnp.broadcast_to(m0, m_scr.shape)
    l_scr[...] = jnp.broadcast_to(jnp.sum(p, axis=-1, keepdims=True), l_scr.shape)
    acc_scr[...] = jnp.dot(p.astype(BF16), v_own, preferred_element_type=F32)

    for n in range(nb - 1):
        @pl.when(n < qi)
        def _(n=n):
            kb = kn_scr[n * BK:(n + 1) * BK, :]
            vb = v_ref[n * BK:(n + 1) * BK, :]
            sn = lax.dot_general(qb, kb, (((1,), (1,)), ((), ())),
                                 preferred_element_type=F32) * scale
            sn = jnp.where(sel[:, n:n + 1] > 0.0, sn, NEG_INF)
            m_old = m_scr[:, :1]
            m_new = jnp.maximum(m_old, jnp.max(sn, axis=-1, keepdims=True))
            alpha = jnp.exp(m_old - m_new)
            pn = jnp.exp(sn - m_new)
            l_scr[...] = alpha * l_scr[...] + jnp.sum(pn, axis=-1, keepdims=True)
            acc_scr[...] = alpha * acc_scr[...] + jnp.dot(pn.astype(BF16), vb,
                                                          preferred_element_type=F32)
            m_scr[...] = jnp.broadcast_to(m_new, m_scr.shape)

    out_ref[...] = (acc_scr[...] / l_scr[:, :1]).astype(out_ref.dtype)


def _moba(z, g_q, g_k, batch, seq):
    t = z.shape[0]
    BK = MOBA_BLOCK
    nq = seq // BK
    dh = MOBA_HEAD_DIM
    return pl.pallas_call(
        _moba_kernel,
        grid=(batch, MOBA_HEADS, nq),
        in_specs=[pl.BlockSpec((BK, dh), lambda b, h, i: (b * nq + i, Z_QB // dh + h)),
                  pl.BlockSpec((seq, dh), lambda b, h, i: (b, Z_KB // dh + h)),
                  pl.BlockSpec((seq, dh), lambda b, h, i: (b, Z_VB // dh + h)),
                  pl.BlockSpec((1, dh), lambda b, h, i: (0, 0)),
                  pl.BlockSpec((1, dh), lambda b, h, i: (0, 0))],
        out_specs=pl.BlockSpec((BK, dh), lambda b, h, i: (b * nq + i, h)),
        out_shape=jax.ShapeDtypeStruct((t, MOBA_WIDTH), BF16),
        scratch_shapes=[pltpu.VMEM((seq, dh), BF16),
                        pltpu.VMEM((LANES, dh), F32),
                        pltpu.VMEM((BK, LANES), F32),
                        pltpu.VMEM((BK, LANES), F32),
                        pltpu.VMEM((BK, dh), F32)],
        compiler_params=_params("arbitrary", "arbitrary", "arbitrary"),
        name="moba",
    )(z, z, z, g_q, g_k)


def _mix_kernel(hm_ref, ob_ref, ga_ref, gb_ref, wa_ref, wb_ref, wo_ref, x_ref, mod_ref,
                out_ref, y_scr):
    @pl.when(pl.program_id(1) == 0)
    def _():
        ya = jnp.dot(hm_ref[...], wa_ref[...], preferred_element_type=F32)
        yb = jnp.dot(ob_ref[...], wb_ref[...], preferred_element_type=F32)
        y = (_sigmoid(ga_ref[...].astype(F32)) * ya + _sigmoid(gb_ref[...].astype(F32)) * yb)
        y_scr[...] = y.astype(BF16)

    proj = jnp.dot(y_scr[...], wo_ref[...], preferred_element_type=F32)
    out_ref[...] = x_ref[...] + mod_ref[2:3, :] * proj


def _mix_out(hm, ob, z, wa, wb, wo, x2, mod3, seq):
    t, d = x2.shape
    tm, tn = 512, 512
    per_b = seq // tm
    return pl.pallas_call(
        _mix_kernel,
        grid=(t // tm, d // tn),
        in_specs=[pl.BlockSpec((tm, MLSTM_WIDTH), lambda i, j: (i, 0)),
                  pl.BlockSpec((tm, MOBA_WIDTH), lambda i, j: (i, 0)),
                  pl.BlockSpec((tm, d), lambda i, j: (i, Z_GA // d)),
                  pl.BlockSpec((tm, d), lambda i, j: (i, Z_GB // d)),
                  pl.BlockSpec((MLSTM_WIDTH, d), lambda i, j: (0, 0)),
                  pl.BlockSpec((MOBA_WIDTH, d), lambda i, j: (0, 0)),
                  pl.BlockSpec((d, tn), lambda i, j: (0, j)),
                  pl.BlockSpec((tm, tn), lambda i, j: (i, j)),
                  pl.BlockSpec((None, 8, tn), lambda i, j: (i // per_b, 0, j))],
        out_specs=pl.BlockSpec((tm, tn), lambda i, j: (i, j)),
        out_shape=jax.ShapeDtypeStruct((t, d), F32),
        scratch_shapes=[pltpu.VMEM((tm, d), BF16)],
        compiler_params=_params("arbitrary", "arbitrary"),
        name="mix_out",
    )(hm, ob, z, z, wa, wb, wo, x2, mod3)


def _route_kernel(x_ref, mod_ref, g_ref, wr_ref, br_ref, t_ref, comb_ref):
    y = _rms(x_ref[...]) * g_ref[...]
    tt = y * (1.0 + mod_ref[4:5, :]) + mod_ref[3:4, :]
    t_ref[...] = tt.astype(t_ref.dtype)
    logits = jnp.dot(tt, wr_ref[...], preferred_element_type=F32,
                     precision=lax.Precision.HIGHEST) + br_ref[...]
    tm = logits.shape[0]
    lane = lax.broadcasted_iota(jnp.int32, (tm, LANES), 1)
    work = logits
    picked = jnp.zeros((tm, LANES), jnp.bool_)
    top = None
    for _ in range(TOP_K):
        mx = jnp.max(work, axis=-1, keepdims=True)
        if top is None:
            top = mx
        first = jnp.min(jnp.where(work == mx, lane, LANES), axis=-1, keepdims=True)
        hit = lane == first
        picked = picked | hit
        work = jnp.where(hit, -jnp.inf, work)
    e = jnp.where(picked, jnp.exp(logits - top), 0.0)
    comb_ref[...] = e / jnp.sum(e, axis=-1, keepdims=True)


def _route(x1, mod3, g_ffn, w_router_pad, b_router_pad, seq):
    t, d = x1.shape
    tm = 512
    per_b = seq // tm
    return pl.pallas_call(
        _route_kernel,
        grid=(t // tm,),
        in_specs=[pl.BlockSpec((tm, d), lambda i: (i, 0)),
                  pl.BlockSpec((None, 8, d), lambda i: (i // per_b, 0, 0)),
                  pl.BlockSpec((1, d), lambda i: (0, 0)),
                  pl.BlockSpec((d, LANES), lambda i: (0, 0)),
                  pl.BlockSpec((1, LANES), lambda i: (0, 0))],
        out_specs=[pl.BlockSpec((tm, d), lambda i: (i, 0)),
                   pl.BlockSpec((tm, LANES), lambda i: (i, 0))],
        out_shape=[jax.ShapeDtypeStruct((t, d), BF16),
                   jax.ShapeDtypeStruct((t, LANES), F32)],
        compiler_params=_params("arbitrary"),
        name="route",
    )(x1, mod3, g_ffn, w_router_pad, b_router_pad)


def _moe_dense_kernel(t_ref, comb_ref, wg_ref, wl_ref, bg_ref, bl_ref, wd_ref, bd_ref,
                      x_ref, mod_ref, out_ref, acc_scr):
    e = pl.program_id(1)
    j = pl.program_id(2)
    ne = pl.num_programs(1)
    nj = pl.num_programs(2)

    @pl.when((e == 0) & (j == 0))
    def _():
        acc_scr[...] = jnp.zeros_like(acc_scr)

    tm = t_ref.shape[0]
    lane = lax.broadcasted_iota(jnp.int32, (tm, LANES), 1)
    cw = jnp.sum(jnp.where(lane == e, comb_ref[...], 0.0), axis=-1, keepdims=True)
    tt = t_ref[...]
    glu = jnp.dot(tt, wg_ref[...].astype(BF16), preferred_element_type=F32) + bg_ref[...]
    lin = jnp.dot(tt, wl_ref[...].astype(BF16), preferred_element_type=F32) + bl_ref[...]
    glu = jnp.minimum(glu, SWIGLU_LIMIT)
    lin = jnp.clip(lin, -SWIGLU_LIMIT, SWIGLU_LIMIT)
    act = glu * _sigmoid(SWIGLU_ALPHA * glu) * (lin + 1.0)
    part = jnp.dot(act.astype(BF16), wd_ref[...].astype(BF16), preferred_element_type=F32)
    bias = jnp.where(j == 0, 1.0, 0.0) * bd_ref[...]
    acc_scr[...] += cw * (part + bias)

    @pl.when((e == ne - 1) & (j == nj - 1))
    def _():
        out_ref[...] = x_ref[...] + mod_ref[5:6, :] * acc_scr[...]


def _moe_dense(tb, comb, w_up, b_up, w_down, b_down, x1, mod3, seq):
    t, d = x1.shape
    ne, _, two_de = w_up.shape
    de = two_de // 2
    tm, tj = 512, 512
    nj = de // tj
    per_b = seq // tm
    return pl.pallas_call(
        _moe_dense_kernel,
        grid=(t // tm, ne, nj),
        in_specs=[pl.BlockSpec((tm, d), lambda i, e, j: (i, 0)),
                  pl.BlockSpec((tm, LANES), lambda i, e, j: (i, 0)),
                  pl.BlockSpec((None, d, tj), lambda i, e, j: (e, 0, j)),
                  pl.BlockSpec((None, d, tj), lambda i, e, j: (e, 0, nj + j)),
                  pl.BlockSpec((None, 1, tj), lambda i, e, j: (e, 0, j)),
                  pl.BlockSpec((None, 1, tj), lambda i, e, j: (e, 0, nj + j)),
                  pl.BlockSpec((None, tj, d), lambda i, e, j: (e, j, 0)),
                  pl.BlockSpec((None, 1, d), lambda i, e, j: (e, 0, 0)),
                  pl.BlockSpec((tm, d), lambda i, e, j: (i, 0)),
                  pl.BlockSpec((None, 8, d), lambda i, e, j: (i // per_b, 0, 0))],
        out_specs=pl.BlockSpec((tm, d), lambda i, e, j: (i, 0)),
        out_shape=jax.ShapeDtypeStruct((t, d), F32),
        scratch_shapes=[pltpu.VMEM((tm, d), F32)],
        compiler_params=_params("arbitrary", "arbitrary", "arbitrary"),
        name="moe_dense",
    )(tb, comb, w_up, w_up, b_up.reshape(ne, 1, two_de), b_up.reshape(ne, 1, two_de),
      w_down, b_down.reshape(ne, 1, d), x1, mod3)


def _layer(x2, c, batch, seq, w_ada, b_ada, g_mix, w_in, b_igate, b_fgate, g_mlstm_out, g_q, g_k,
           w_branch_a, w_branch_b, w_out, g_ffn, w_router, b_router, w_up, b_up, w_down, b_down):
    d = D_MODEL
    c_pad = jnp.zeros((8, d), F32).at[:batch].set(c)
    mod = _ada(c_pad, w_ada, b_ada)
    mod3 = jnp.pad(mod[:batch].reshape(batch, 6, d), ((0, 0), (0, 2), (0, 0)))

    i0 = Z_QB
    w_main = jnp.concatenate([w_in[:, :i0], w_in[:, i0 + 2 * MLSTM_HEADS:]], axis=1).astype(BF16)
    w_gate = jnp.pad(w_in[:, i0:i0 + 2 * MLSTM_HEADS],
                     ((0, 0), (0, LANES - 2 * MLSTM_HEADS))).astype(BF16)
    gbias = jnp.pad(jnp.concatenate([b_igate, b_fgate]), (0, LANES - 2 * MLSTM_HEADS)).reshape(1, LANES)

    z, gates = _in_proj(x2, mod3, g_mix.reshape(1, d), w_main, w_gate, seq)
    hm = _mlstm(z, gates, gbias, g_mlstm_out, batch, seq)
    ob = _moba(z, g_q.reshape(1, -1), g_k.reshape(1, -1), batch, seq)
    x1 = _mix_out(hm, ob, z, w_branch_a.astype(BF16), w_branch_b.astype(BF16), w_out.astype(BF16),
                  x2, mod3, seq)

    wr = jnp.pad(w_router, ((0, 0), (0, LANES - N_EXPERTS)))
    br = jnp.concatenate([b_router, jnp.full((LANES - N_EXPERTS,), NEG_INF, F32)]).reshape(1, LANES)
    tb, comb = _route(x1, mod3, g_ffn.reshape(1, d), wr, br, seq)
    return _moe_dense(tb, comb, w_up, b_up, w_down, b_down, x1, mod3, seq)


def kernel(x, c, w_ada, b_ada, g_mix, w_in, b_igate, b_fgate, g_mlstm_out, g_q, g_k, w_branch_a,
           w_branch_b, w_out, g_ffn, w_router, b_router, w_up, b_up, w_down, b_down):
    batch, seq, d = x.shape
    x2 = x.reshape(batch * seq, d)
    for l in range(w_ada.shape[0]):
        x2 = _layer(x2, c, batch, seq, w_ada[l], b_ada[l], g_mix[l], w_in[l], b_igate[l], b_fgate[l],
                    g_mlstm_out[l], g_q[l], g_k[l], w_branch_a[l], w_branch_b[l], w_out[l], g_ffn[l],
                    w_router[l], b_router[l], w_up[l], b_up[l], w_down[l], b_down[l])
    return x2.reshape(batch, seq, d)
```

```python
import functools

import jax
import jax.numpy as jnp
from jax import lax
from jax.experimental import pallas as pl
from jax.experimental.pallas import tpu as pltpu

F32 = jnp.float32
BF16 = jnp.bfloat16
I32 = jnp.int32

D_MODEL = 2048
MLSTM_HEADS = 8
MLSTM_DV = 128
MLSTM_DK = 64
MLSTM_WIDTH = MLSTM_HEADS * MLSTM_DV
GATE_SOFTCAP = 15.0
MOBA_HEADS = 8
MOBA_HEAD_DIM = 128
MOBA_BLOCK = 256
MOBA_TOPK = 3
MOBA_WIDTH = MOBA_HEADS * MOBA_HEAD_DIM
N_EXPERTS = 32
TOP_K = 4
D_EXPERT = D_MODEL
SWIGLU_ALPHA = 1.702
SWIGLU_LIMIT = 7.0
NORM_EPS = 1e-6
NEG_INF = -1e30

LANES = 128
VMEM_LIMIT = 56 * 1024 * 1024

Z_QA, Z_KA, Z_VA, Z_OA = 0, 512, 1024, 2048
Z_QB, Z_KB, Z_VB = 3072, 4096, 5120
Z_GA, Z_GB = 6144, 8192
Z_WIDTH = 10240

MLSTM_CHUNK = 256

MOE_TM = 256
ROW_CHUNKS = D_MODEL // LANES


def _params(*sem):
    return pltpu.CompilerParams(dimension_semantics=sem, vmem_limit_bytes=VMEM_LIMIT)


def _rms(x, eps=NORM_EPS):
    return x * lax.rsqrt(jnp.mean(x * x, axis=-1, keepdims=True) + eps)


def _sigmoid(x):
    return 1.0 / (1.0 + jnp.exp(-x))


def _ada_kernel(c_ref, w_ref, b_ref, o_ref):
    c = c_ref[...]
    s = (c * _sigmoid(c)).astype(BF16)
    o_ref[...] = jnp.dot(s, w_ref[...].astype(BF16), preferred_element_type=F32) + b_ref[...]


def _ada(c_pad, w_ada, b_ada):
    rows, d = c_pad.shape
    n = w_ada.shape[1]
    tn = 1024
    return pl.pallas_call(
        _ada_kernel,
        grid=(n // tn,),
        in_specs=[pl.BlockSpec((rows, d), lambda j: (0, 0)),
                  pl.BlockSpec((d, tn), lambda j: (0, j)),
                  pl.BlockSpec((1, tn), lambda j: (0, j))],
        out_specs=pl.BlockSpec((rows, tn), lambda j: (0, j)),
        out_shape=jax.ShapeDtypeStruct((rows, n), F32),
        compiler_params=_params("arbitrary"),
        name="ada",
    )(c_pad, w_ada, b_ada.reshape(1, n))


def _in_kernel(x_ref, mod_ref, g_ref, w_ref, wg_ref, z_ref, gates_ref, h_scr):
    @pl.when(pl.program_id(1) == 0)
    def _():
        y = _rms(x_ref[...]) * g_ref[...]
        h = y * (1.0 + mod_ref[1:2, :]) + mod_ref[0:1, :]
        hb = h.astype(BF16)
        h_scr[...] = hb
        gates_ref[...] = jnp.dot(hb, wg_ref[...], preferred_element_type=F32)

    z_ref[...] = jnp.dot(h_scr[...], w_ref[...], preferred_element_type=F32).astype(z_ref.dtype)


def _in_proj(x2, mod3, g_mix, w_main, w_gate, seq):
    t, d = x2.shape
    n = w_main.shape[1]
    tm, tn = 512, 1024
    per_b = seq // tm
    return pl.pallas_call(
        _in_kernel,
        grid=(t // tm, n // tn),
        in_specs=[pl.BlockSpec((tm, d), lambda i, j: (i, 0)),
                  pl.BlockSpec((None, 8, d), lambda i, j: (i // per_b, 0, 0)),
                  pl.BlockSpec((1, d), lambda i, j: (0, 0)),
                  pl.BlockSpec((d, tn), lambda i, j: (0, j)),
                  pl.BlockSpec((d, LANES), lambda i, j: (0, 0))],
        out_specs=[pl.BlockSpec((tm, tn), lambda i, j: (i, j)),
                   pl.BlockSpec((tm, LANES), lambda i, j: (i, 0))],
        out_shape=[jax.ShapeDtypeStruct((t, n), BF16),
                   jax.ShapeDtypeStruct((t, LANES), F32)],
        scratch_shapes=[pltpu.VMEM((tm, d), BF16)],
        compiler_params=_params("arbitrary", "arbitrary"),
        name="in_proj",
    )(x2, mod3, g_mix, w_main, w_gate)


def _mlstm_kernel(q_ref, k_ref, v_ref, o_ref, gates_ref, gbias_ref, gout_ref, out_ref,
                  c_scr, n_scr, m_scr):
    L = MLSTM_CHUNK
    pair = pl.program_id(1)

    @pl.when(pl.program_id(2) == 0)
    def _():
        c_scr[...] = jnp.zeros_like(c_scr)
        n_scr[...] = jnp.zeros_like(n_scr)
        m_scr[...] = jnp.zeros_like(m_scr)

    pre = gates_ref[...] + gbias_ref[...]
    capped = GATE_SOFTCAP * jnp.tanh(pre / GATE_SOFTCAP)
    log_f = jnp.minimum(capped, 0.0) - jnp.log1p(jnp.exp(-jnp.abs(capped)))
    row = lax.broadcasted_iota(jnp.int32, (L, L), 0)
    col = lax.broadcasted_iota(jnp.int32, (L, L), 1)
    causal = col <= row
    tril = jnp.where(causal, 1.0, 0.0).astype(F32)
    b_all = jnp.dot(tril, log_f, preferred_element_type=F32,
                    precision=lax.Precision.HIGHEST)
    li_t = capped.T
    b_t = b_all.T

    for j in range(2):
        head = 2 * pair + j
        q = q_ref[:, j * MLSTM_DK:(j + 1) * MLSTM_DK] * (MLSTM_DK ** -0.5)
        k = k_ref[:, j * MLSTM_DK:(j + 1) * MLSTM_DK]
        v = v_ref[:, j * MLSTM_DV:(j + 1) * MLSTM_DV]
        lane = lax.broadcasted_iota(jnp.int32, (L, LANES), 1)
        sub = lax.broadcasted_iota(jnp.int32, (LANES, L), 0)
        li_c = jnp.sum(jnp.where(lane == head, capped, 0.0), axis=1, keepdims=True)
        b_c = jnp.sum(jnp.where(lane == head + MLSTM_HEADS, b_all, 0.0), axis=1, keepdims=True)
        li_r = jnp.sum(jnp.where(sub == head, li_t, 0.0), axis=0, keepdims=True)
        b_r = jnp.sum(jnp.where(sub == head + MLSTM_HEADS, b_t, 0.0), axis=0, keepdims=True)
        ct = c_scr[j]
        nvec = n_scr[j]
        m_in = m_scr[j][:, :1]
        g = b_c[L - 1:L, :]

        d = jnp.where(causal, b_c - b_r + li_r, NEG_INF)
        dmax = jnp.max(d, axis=-1, keepdims=True)
        inter = b_c + m_in
        m_j = jnp.maximum(inter, dmax)
        p = jnp.exp(d - m_j)
        s = lax.dot_general(q, k, (((1,), (1,)), ((), ())), preferred_element_type=F32) * p
        s_inter = jnp.exp(inter - m_j)
        qf = q.astype(F32)
        num = (jnp.dot(s.astype(BF16), v, preferred_element_type=F32)
               + s_inter * jnp.dot(q, ct.astype(BF16), preferred_element_type=F32))
        den = (jnp.sum(s, axis=-1, keepdims=True)
               + s_inter * jnp.sum(qf * nvec, axis=-1, keepdims=True))
        h = num / jnp.maximum(jnp.abs(den), jnp.exp(-m_j))

        gout = gout_ref[pl.ds(head, 1), :]
        hn = _rms(h) * gout
        og = o_ref[:, j * MLSTM_DV:(j + 1) * MLSTM_DV].astype(F32)
        out_ref[:, j * MLSTM_DV:(j + 1) * MLSTM_DV] = (hn * _sigmoid(og)).astype(out_ref.dtype)

        a = g - b_c + li_c
        m_loc = jnp.max(a, axis=0, keepdims=True)
        w = jnp.exp(a - m_loc)
        kw = k.astype(F32) * w
        c_loc = lax.dot_general(kw.astype(BF16), v, (((0,), (0,)), ((), ())),
                                preferred_element_type=F32)
        n_loc = jnp.sum(kw, axis=0, keepdims=True)
        m_new = jnp.maximum(g + m_in, m_loc)
        s_prev = jnp.exp(g + m_in - m_new)
        s_loc = jnp.exp(m_loc - m_new)
        c_scr[j] = s_prev * ct + s_loc * c_loc
        n_scr[j] = s_prev * nvec + s_loc * n_loc
        m_scr[j] = jnp.broadcast_to(m_new, (1, LANES))


def _mlstm(z, gates, gbias, g_out, batch, seq):
    t = z.shape[0]
    L = MLSTM_CHUNK
    nc = seq // L
    pairs = MLSTM_HEADS // 2
    rowmap = lambda b, p, c: b * nc + c
    return pl.pallas_call(
        _mlstm_kernel,
        grid=(batch, pairs, nc),
        in_specs=[pl.BlockSpec((L, 128), lambda b, p, c: (rowmap(b, p, c), Z_QA // 128 + p)),
                  pl.BlockSpec((L, 128), lambda b, p, c: (rowmap(b, p, c), Z_KA // 128 + p)),
                  pl.BlockSpec((L, 256), lambda b, p, c: (rowmap(b, p, c), Z_VA // 256 + p)),
                  pl.BlockSpec((L, 256), lambda b, p, c: (rowmap(b, p, c), Z_OA // 256 + p)),
                  pl.BlockSpec((L, LANES), lambda b, p, c: (rowmap(b, p, c), 0)),
                  pl.BlockSpec((1, LANES), lambda b, p, c: (0, 0)),
                  pl.BlockSpec((MLSTM_HEADS, MLSTM_DV), lambda b, p, c: (0, 0))],
        out_specs=pl.BlockSpec((L, 256), lambda b, p, c: (rowmap(b, p, c), p)),
        out_shape=jax.ShapeDtypeStruct((t, MLSTM_WIDTH), BF16),
        scratch_shapes=[pltpu.VMEM((2, MLSTM_DK, MLSTM_DV), F32),
                        pltpu.VMEM((2, 1, MLSTM_DK), F32),
                        pltpu.VMEM((2, 1, LANES), F32)],
        compiler_params=_params("arbitrary", "arbitrary", "arbitrary"),
        name="mlstm",
    )(z, z, z, z, gates, gbias, g_out)


def _moba_kernel(q_ref, k_ref, v_ref, gq_ref, gk_ref, out_ref, kn_scr, kmean_scr):
    BK = MOBA_BLOCK
    nb = k_ref.shape[0] // BK
    qi = pl.program_id(2)
    scale = MOBA_HEAD_DIM ** -0.5

    @pl.when(qi == 0)
    def _():
        kmean_scr[...] = jnp.zeros_like(kmean_scr)
        for n in range(nb):
            kf = k_ref[n * BK:(n + 1) * BK, :].astype(F32)
            kn = _rms(kf) * gk_ref[...]
            kn_scr[n * BK:(n + 1) * BK, :] = kn.astype(BF16)
            kmean_scr[n:n + 1, :] = jnp.mean(kn, axis=0, keepdims=True)

    qn = _rms(q_ref[...].astype(F32)) * gq_ref[...]
    qb = qn.astype(BF16)
    gate = lax.dot_general(qn, kmean_scr[...], (((1,), (1,)), ((), ())),
                           preferred_element_type=F32,
                           precision=lax.Precision.HIGHEST)
    lane = lax.broadcasted_iota(jnp.int32, (BK, LANES), 1)
    cnt = jnp.zeros((BK, LANES), F32)
    for m in range(nb):
        gm = gate[:, m:m + 1]
        beats = (gm > gate) | ((gm == gate) & (m < lane))
        cnt = cnt + jnp.where(beats & (m < qi), 1.0, 0.0)
    sel = jnp.where((cnt < MOBA_TOPK) & (lane < qi), 1.0, 0.0)

    row = lax.broadcasted_iota(jnp.int32, (BK, BK), 0)
    col = lax.broadcasted_iota(jnp.int32, (BK, BK), 1)
    causal = jnp.where(col <= row, 1.0, 0.0)

    def attend(n_blocks):
        nk = n_blocks * BK
        s = lax.dot_general(qb, kn_scr[0:nk, :], (((1,), (1,)), ((), ())),
                            preferred_element_type=F32) * scale
        allowed = jnp.concatenate(
            [jnp.where(qi == n, causal, jnp.broadcast_to(sel[:, n:n + 1], (BK, BK)))
             for n in range(n_blocks)], axis=1)
        s = jnp.where(allowed > 0.0, s, NEG_INF)
        p = jnp.exp(s - jnp.max(s, axis=-1, keepdims=True))
        num = jnp.dot(p.astype(BF16), v_ref[0:nk, :], preferred_element_type=F32)
        out_ref[...] = (num / jnp.sum(p, axis=-1, keepdims=True)).astype(out_ref.dtype)

    half = nb // 2

    @pl.when(qi < half)
    def _():
        attend(half)

    @pl.when(qi >= half)
    def _():
        attend(nb)


def _moba(z, g_q, g_k, batch, seq):
    t = z.shape[0]
    BK = MOBA_BLOCK
    nq = seq // BK
    dh = MOBA_HEAD_DIM
    return pl.pallas_call(
        _moba_kernel,
        grid=(batch, MOBA_HEADS, nq),
        in_specs=[pl.BlockSpec((BK, dh), lambda b, h, i: (b * nq + i, Z_QB // dh + h)),
                  pl.BlockSpec((seq, dh), lambda b, h, i: (b, Z_KB // dh + h)),
                  pl.BlockSpec((seq, dh), lambda b, h, i: (b, Z_VB // dh + h)),
                  pl.BlockSpec((1, dh), lambda b, h, i: (0, 0)),
                  pl.BlockSpec((1, dh), lambda b, h, i: (0, 0))],
        out_specs=pl.BlockSpec((BK, dh), lambda b, h, i: (b * nq + i, h)),
        out_shape=jax.ShapeDtypeStruct((t, MOBA_WIDTH), BF16),
        scratch_shapes=[pltpu.VMEM((seq, dh), BF16),
                        pltpu.VMEM((LANES, dh), F32)],
        compiler_params=_params("arbitrary", "arbitrary", "arbitrary"),
        name="moba",
    )(z, z, z, g_q, g_k)


def _mix_kernel(hm_ref, ob_ref, ga_ref, gb_ref, wa_ref, wb_ref, wo_ref, x_ref, mod_ref,
                out_ref, y_scr):
    @pl.when(pl.program_id(1) == 0)
    def _():
        ya = jnp.dot(hm_ref[...], wa_ref[...], preferred_element_type=F32)
        yb = jnp.dot(ob_ref[...], wb_ref[...], preferred_element_type=F32)
        y = (_sigmoid(ga_ref[...].astype(F32)) * ya + _sigmoid(gb_ref[...].astype(F32)) * yb)
        y_scr[...] = y.astype(BF16)

    proj = jnp.dot(y_scr[...], wo_ref[...], preferred_element_type=F32)
    out_ref[...] = x_ref[...] + mod_ref[2:3, :] * proj


def _mix_out(hm, ob, z, wa, wb, wo, x2, mod3, seq):
    t, d = x2.shape
    tm, tn = 512, 512
    per_b = seq // tm
    return pl.pallas_call(
        _mix_kernel,
        grid=(t // tm, d // tn),
        in_specs=[pl.BlockSpec((tm, MLSTM_WIDTH), lambda i, j: (i, 0)),
                  pl.BlockSpec((tm, MOBA_WIDTH), lambda i, j: (i, 0)),
                  pl.BlockSpec((tm, d), lambda i, j: (i, Z_GA // d)),
                  pl.BlockSpec((tm, d), lambda i, j: (i, Z_GB // d)),
                  pl.BlockSpec((MLSTM_WIDTH, d), lambda i, j: (0, 0)),
                  pl.BlockSpec((MOBA_WIDTH, d), lambda i, j: (0, 0)),
                  pl.BlockSpec((d, tn), lambda i, j: (0, j)),
                  pl.BlockSpec((tm, tn), lambda i, j: (i, j)),
                  pl.BlockSpec((None, 8, tn), lambda i, j: (i // per_b, 0, j))],
        out_specs=pl.BlockSpec((tm, tn), lambda i, j: (i, j)),
        out_shape=jax.ShapeDtypeStruct((t, d), F32),
        scratch_shapes=[pltpu.VMEM((tm, d), BF16)],
        compiler_params=_params("arbitrary", "arbitrary"),
        name="mix_out",
    )(hm, ob, z, z, wa, wb, wo, x2, mod3)


def _route_kernel(x_ref, mod_ref, g_ref, wr_ref, br_ref, t_ref, ri_ref, rw_ref, cnt_ref, cnt_scr):
    i = pl.program_id(0)

    @pl.when(i == 0)
    def _():
        cnt_scr[...] = jnp.zeros_like(cnt_scr)

    y = _rms(x_ref[...]) * g_ref[...]
    tt = y * (1.0 + mod_ref[4:5, :]) + mod_ref[3:4, :]
    tm = tt.shape[0]
    for s in range(ROW_CHUNKS):
        t_ref[pl.ds(s, tm, stride=ROW_CHUNKS), :] = tt[:, s * LANES:(s + 1) * LANES]

    logits = jnp.dot(tt, wr_ref[...], preferred_element_type=F32,
                     precision=lax.Precision.HIGHEST) + br_ref[...]
    lane = lax.broadcasted_iota(I32, (tm, LANES), 1)
    work = logits
    picks = []
    for _ in range(TOP_K):
        mx = jnp.max(work, axis=-1, keepdims=True)
        first = jnp.min(jnp.where(work == mx, lane, LANES), axis=-1, keepdims=True)
        picks.append((first, mx))
        work = jnp.where(lane == first, -jnp.inf, work)
    top = picks[0][1]
    exps = [jnp.exp(v - top) for _, v in picks]
    denom = exps[0] + exps[1] + exps[2] + exps[3]

    sel = jnp.zeros((tm, LANES), F32)
    for first, _ in picks:
        sel = sel + jnp.where(lane == first, 1.0, 0.0)
    row = lax.broadcasted_iota(I32, (tm, tm), 0)
    col = lax.broadcasted_iota(I32, (tm, tm), 1)
    strict = jnp.where(col < row, 1.0, 0.0).astype(BF16)
    before = jnp.dot(strict, sel.astype(BF16), preferred_element_type=F32) + cnt_scr[...]

    ri = jnp.zeros((tm, LANES), I32)
    rw = jnp.zeros((tm, LANES), F32)
    for kk, (first, _) in enumerate(picks):
        rank = jnp.sum(jnp.where(lane == first, before, 0.0), axis=-1, keepdims=True).astype(I32)
        ri = jnp.where(lane == kk, first, ri)
        ri = jnp.where(lane == TOP_K + kk, rank, ri)
        rw = jnp.where(lane == kk, exps[kk] / denom, rw)
    ri_ref[...] = ri
    rw_ref[...] = rw
    cnt_scr[...] = cnt_scr[...] + jnp.sum(sel, axis=0, keepdims=True)
    cnt_ref[...] = jnp.broadcast_to(cnt_scr[...], cnt_ref.shape)


def _route(x1, mod3, g_ffn, w_router_pad, b_router_pad, seq):
    t, d = x1.shape
    tm = 512
    per_b = seq // tm
    return pl.pallas_call(
        _route_kernel,
        grid=(t // tm,),
        in_specs=[pl.BlockSpec((tm, d), lambda i: (i, 0)),
                  pl.BlockSpec((None, 8, d), lambda i: (i // per_b, 0, 0)),
                  pl.BlockSpec((1, d), lambda i: (0, 0)),
                  pl.BlockSpec((d, LANES), lambda i: (0, 0)),
                  pl.BlockSpec((1, LANES), lambda i: (0, 0))],
        out_specs=[pl.BlockSpec((tm * ROW_CHUNKS, LANES), lambda i: (i, 0)),
                   pl.BlockSpec((tm, LANES), lambda i: (i, 0)),
                   pl.BlockSpec((tm, LANES), lambda i: (i, 0)),
                   pl.BlockSpec((8, LANES), lambda i: (0, 0))],
        out_shape=[jax.ShapeDtypeStruct((t * ROW_CHUNKS, LANES), F32),
                   jax.ShapeDtypeStruct((t, LANES), I32),
                   jax.ShapeDtypeStruct((t, LANES), F32),
                   jax.ShapeDtypeStruct((8, LANES), F32)],
        scratch_shapes=[pltpu.VMEM((1, LANES), F32)],
        compiler_params=_params("arbitrary"),
        name="route",
    )(x1, mod3, g_ffn, w_router_pad, b_router_pad)


def _inv_kernel(pos_ref, src_ref):
    n_rows = src_ref.shape[0]
    n_pairs = pos_ref.shape[0]

    def zero(i, c):
        src_ref[i] = 0
        return c

    lax.fori_loop(0, n_rows, zero, 0, unroll=8)

    def put(i, c):
        src_ref[pos_ref[i]] = lax.shift_right_logical(i, 2)
        return c

    lax.fori_loop(0, n_pairs, put, 0, unroll=8)


def _inverse(pos_flat, n_rows):
    return pl.pallas_call(
        _inv_kernel,
        in_specs=[pl.BlockSpec(memory_space=pltpu.SMEM)],
        out_specs=pl.BlockSpec(memory_space=pltpu.SMEM),
        out_shape=jax.ShapeDtypeStruct((n_rows,), I32),
        name="inverse",
    )(pos_flat)


def _row_copy(src_hbm, tok, buf, slot, r, sem):
    return pltpu.make_async_copy(src_hbm.at[pl.ds(tok * ROW_CHUNKS, ROW_CHUNKS), :],
                                 buf.at[slot, pl.ds(r * ROW_CHUNKS, ROW_CHUNKS), :],
                                 sem.at[slot])


def _dispatch_kernel(src_ref, nt_ref, t_hbm, out_ref, buf, sem):
    i = pl.program_id(0)
    nt = nt_ref[0]
    tm = out_ref.shape[0]

    def issue(tile, slot):
        def body(r, c):
            tok = src_ref[tile * tm + r]
            _row_copy(t_hbm, tok, buf, slot, r, sem).start()
            return c
        lax.fori_loop(0, tm, body, 0, unroll=8)

    @pl.when(i == 0)
    def _():
        issue(0, 0)

    @pl.when(i + 1 < nt)
    def _():
        issue(i + 1, (i + 1) % 2)

    @pl.when(i < nt)
    def _():
        slot = i % 2
        pltpu.make_async_copy(t_hbm.at[pl.ds(0, tm * ROW_CHUNKS), :], buf.at[slot], sem.at[slot]).wait()
        for s in range(ROW_CHUNKS):
            chunk = buf[slot, pl.ds(s, tm, stride=ROW_CHUNKS), :]
            out_ref[:, s * LANES:(s + 1) * LANES] = chunk.astype(out_ref.dtype)

    @pl.when(i >= nt)
    def _():
        out_ref[...] = jnp.zeros_like(out_ref)


def _dispatch(src, nt, t_rows, n_rows):
    tm = MOE_TM
    grid_spec = pltpu.PrefetchScalarGridSpec(
        num_scalar_prefetch=2,
        grid=(n_rows // tm,),
        in_specs=[pl.BlockSpec(memory_space=pl.ANY)],
        out_specs=pl.BlockSpec((tm, D_MODEL), lambda i, src, nt: (i, 0)),
        scratch_shapes=[pltpu.VMEM((2, tm * ROW_CHUNKS, LANES), F32),
                        pltpu.SemaphoreType.DMA((2,))],
    )
    return pl.pallas_call(
        _dispatch_kernel,
        grid_spec=grid_spec,
        out_shape=jax.ShapeDtypeStruct((n_rows, D_MODEL), BF16),
        compiler_params=_params("arbitrary"),
        name="dispatch",
    )(src, nt, t_rows)


def _gmm1_kernel(te_ref, nt_ref, x_ref, wg_ref, wl_ref, bg_ref, bl_ref, act_ref, wg_scr, wl_scr):
    i = pl.program_id(1)

    @pl.when(i < nt_ref[0])
    def _():
        new_group = (i == 0) | (te_ref[i] != te_ref[jnp.maximum(i - 1, 0)])

        @pl.when(new_group)
        def _():
            wg_scr[...] = wg_ref[...].astype(BF16)
            wl_scr[...] = wl_ref[...].astype(BF16)

        x = x_ref[...]
        glu = jnp.dot(x, wg_scr[...], preferred_element_type=F32) + bg_ref[...]
        lin = jnp.dot(x, wl_scr[...], preferred_element_type=F32) + bl_ref[...]
        glu = jnp.minimum(glu, SWIGLU_LIMIT)
        lin = jnp.clip(lin, -SWIGLU_LIMIT, SWIGLU_LIMIT)
        act_ref[...] = (glu * _sigmoid(SWIGLU_ALPHA * glu) * (lin + 1.0)).astype(act_ref.dtype)

    @pl.when(i >= nt_ref[0])
    def _():
        act_ref[...] = jnp.zeros_like(act_ref)


def _gmm1(te, nt, xs, w_up, b_up3):
    n_rows, d = xs.shape
    de = w_up.shape[2] // 2
    tm, tj = MOE_TM, 1024
    nj = de // tj
    row = lambda j, i, te, nt: jnp.minimum(i, nt[0] - 1)
    exp = lambda j, i, te, nt: te[jnp.minimum(i, nt[0] - 1)]
    grid_spec = pltpu.PrefetchScalarGridSpec(
        num_scalar_prefetch=2,
        grid=(nj, n_rows // tm),
        in_specs=[pl.BlockSpec((tm, d), lambda j, i, te, nt: (row(j, i, te, nt), 0)),
                  pl.BlockSpec((None, d, tj), lambda j, i, te, nt: (exp(j, i, te, nt), 0, j)),
                  pl.BlockSpec((None, d, tj), lambda j, i, te, nt: (exp(j, i, te, nt), 0, nj + j)),
                  pl.BlockSpec((None, 1, tj), lambda j, i, te, nt: (exp(j, i, te, nt), 0, j)),
                  pl.BlockSpec((None, 1, tj), lambda j, i, te, nt: (exp(j, i, te, nt), 0, nj + j))],
        out_specs=pl.BlockSpec((tm, tj), lambda j, i, te, nt: (i, j)),
        scratch_shapes=[pltpu.VMEM((d, tj), BF16), pltpu.VMEM((d, tj), BF16)],
    )
    return pl.pallas_call(
        _gmm1_kernel,
        grid_spec=grid_spec,
        out_shape=jax.ShapeDtypeStruct((n_rows, de), BF16),
        compiler_params=_params("arbitrary", "arbitrary"),
        name="gmm1",
    )(te, nt, xs, w_up, w_up, b_up3, b_up3)


def _gmm2_kernel(te_ref, nt_ref, a_ref, wd_ref, bd_ref, y_ref, wd_scr):
    i = pl.program_id(0)

    @pl.when(i < nt_ref[0])
    def _():
        new_group = (i == 0) | (te_ref[i] != te_ref[jnp.maximum(i - 1, 0)])

        @pl.when(new_group)
        def _():
            wd_scr[...] = wd_ref[...].astype(BF16)

        y = jnp.dot(a_ref[...], wd_scr[...], preferred_element_type=F32) + bd_ref[...]
        tm = y.shape[0]
        for s in range(ROW_CHUNKS):
            y_ref[pl.ds(s, tm, stride=ROW_CHUNKS), :] = y[:, s * LANES:(s + 1) * LANES]

    @pl.when(i >= nt_ref[0])
    def _():
        y_ref[...] = jnp.zeros_like(y_ref)


def _gmm2(te, nt, act, w_down, b_down3):
    n_rows, de = act.shape
    d = w_down.shape[2]
    tm = MOE_TM
    row = lambda i, te, nt: jnp.minimum(i, nt[0] - 1)
    exp = lambda i, te, nt: te[jnp.minimum(i, nt[0] - 1)]
    grid_spec = pltpu.PrefetchScalarGridSpec(
        num_scalar_prefetch=2,
        grid=(n_rows // tm,),
        in_specs=[pl.BlockSpec((tm, de), lambda i, te, nt: (row(i, te, nt), 0)),
                  pl.BlockSpec((None, de, d), lambda i, te, nt: (exp(i, te, nt), 0, 0)),
                  pl.BlockSpec((None, 1, d), lambda i, te, nt: (exp(i, te, nt), 0, 0))],
        out_specs=pl.BlockSpec((tm * ROW_CHUNKS, LANES), lambda i, te, nt: (i, 0)),
        scratch_shapes=[pltpu.VMEM((de, d), BF16)],
    )
    return pl.pallas_call(
        _gmm2_kernel,
        grid_spec=grid_spec,
        out_shape=jax.ShapeDtypeStruct((n_rows * ROW_CHUNKS, LANES), F32),
        compiler_params=_params("arbitrary"),
        name="gmm2",
    )(te, nt, act, w_down, b_down3)


def _combine_kernel(pos_ref, y_hbm, rw_ref, x_ref, mod_ref, out_ref, buf, sem):
    i = pl.program_id(0)
    n = pl.num_programs(0)
    tm = out_ref.shape[0]

    def issue(tile, slot):
        def body(r, c):
            for kk in range(TOP_K):
                p = pos_ref[(tile * tm + r) * TOP_K + kk]
                pltpu.make_async_copy(y_hbm.at[pl.ds(p * ROW_CHUNKS, ROW_CHUNKS), :],
                                      buf.at[slot, kk, pl.ds(r * ROW_CHUNKS, ROW_CHUNKS), :],
                                      sem.at[slot]).start()
            return c
        lax.fori_loop(0, tm, body, 0, unroll=4)

    @pl.when(i == 0)
    def _():
        issue(0, 0)

    @pl.when(i + 1 < n)
    def _():
        issue(i + 1, (i + 1) % 2)

    slot = i % 2
    for kk in range(TOP_K):
        pltpu.make_async_copy(y_hbm.at[pl.ds(0, tm * ROW_CHUNKS), :], buf.at[slot, kk],
                              sem.at[slot]).wait()
    rw = rw_ref[...]
    gate = mod_ref[5:6, :]
    for s in range(ROW_CHUNKS):
        acc = None
        for kk in range(TOP_K):
            chunk = buf[slot, kk, pl.ds(s, tm, stride=ROW_CHUNKS), :]
            term = rw[:, kk:kk + 1] * chunk
            acc = term if acc is None else acc + term
        sl = slice(s * LANES, (s + 1) * LANES)
        out_ref[:, sl] = x_ref[:, sl] + gate[:, sl] * acc


def _combine(pos_flat, y_rows, rw, x1, mod3, seq):
    t, d = x1.shape
    tm = 256
    per_b = seq // tm
    grid_spec = pltpu.PrefetchScalarGridSpec(
        num_scalar_prefetch=1,
        grid=(t // tm,),
        in_specs=[pl.BlockSpec(memory_space=pl.ANY),
                  pl.BlockSpec((tm, LANES), lambda i, pos: (i, 0)),
                  pl.BlockSpec((tm, d), lambda i, pos: (i, 0)),
                  pl.BlockSpec((None, 8, d), lambda i, pos: (i // per_b, 0, 0))],
        out_specs=pl.BlockSpec((tm, d), lambda i, pos: (i, 0)),
        scratch_shapes=[pltpu.VMEM((2, TOP_K, tm * ROW_CHUNKS, LANES), F32),
                        pltpu.SemaphoreType.DMA((2,))],
    )
    return pl.pallas_call(
        _combine_kernel,
        grid_spec=grid_spec,
        out_shape=jax.ShapeDtypeStruct((t, d), F32),
        compiler_params=_params("arbitrary"),
        name="combine",
    )(pos_flat, y_rows, rw, x1, mod3)


def _moe(x1, mod3, g_ffn, w_router, b_router, w_up, b_up, w_down, b_down, seq):
    t, d = x1.shape
    ne = w_up.shape[0]
    tm = MOE_TM
    n_rows = t * TOP_K + ne * tm
    n_tiles = n_rows // tm

    wr = jnp.pad(w_router, ((0, 0), (0, LANES - ne)))
    br = jnp.concatenate([b_router, jnp.full((LANES - ne,), NEG_INF, F32)]).reshape(1, LANES)
    t_rows, ri, rw, cnt = _route(x1, mod3, g_ffn.reshape(1, d), wr, br, seq)

    counts = cnt[0, :ne].astype(I32)
    tiles_per = (counts + tm - 1) // tm
    tile_end = jnp.cumsum(tiles_per)
    offs = (tile_end - tiles_per) * tm
    eidx = ri[:, :TOP_K]
    rank = ri[:, TOP_K:2 * TOP_K]
    onehot = eidx[..., None] == jnp.arange(ne, dtype=I32)
    pos = (jnp.sum(jnp.where(onehot, offs, 0), axis=-1) + rank).reshape(-1)
    nt = tile_end[-1:].astype(I32)
    te = jnp.minimum(jnp.sum(jnp.arange(n_tiles, dtype=I32)[:, None] >= tile_end[None, :], axis=1),
                     ne - 1).astype(I32)

    src = _inverse(pos, n_rows)
    xs = _dispatch(src, nt, t_rows, n_rows)
    act = _gmm1(te, nt, xs, w_up, b_up.reshape(ne, 1, -1))
    y_rows = _gmm2(te, nt, act, w_down, b_down.reshape(ne, 1, d))
    return _combine(pos, y_rows, rw, x1, mod3, seq)


def _layer(x2, c, batch, seq, w_ada, b_ada, g_mix, w_in, b_igate, b_fgate, g_mlstm_out, g_q, g_k,
           w_branch_a, w_branch_b, w_out, g_ffn, w_router, b_router, w_up, b_up, w_down, b_down):
    d = D_MODEL
    c_pad = jnp.zeros((8, d), F32).at[:batch].set(c)
    mod = _ada(c_pad, w_ada, b_ada)
    mod3 = jnp.pad(mod[:batch].reshape(batch, 6, d), ((0, 0), (0, 2), (0, 0)))

    i0 = Z_QB
    w_main = jnp.concatenate([w_in[:, :i0], w_in[:, i0 + 2 * MLSTM_HEADS:]], axis=1).astype(BF16)
    w_gate = jnp.pad(w_in[:, i0:i0 + 2 * MLSTM_HEADS],
                     ((0, 0), (0, LANES - 2 * MLSTM_HEADS))).astype(BF16)
    gbias = jnp.pad(jnp.concatenate([b_igate, b_fgate]), (0, LANES - 2 * MLSTM_HEADS)).reshape(1, LANES)

    z, gates = _in_proj(x2, mod3, g_mix.reshape(1, d), w_main, w_gate, seq)
    hm = _mlstm(z, gates, gbias, g_mlstm_out, batch, seq)
    ob = _moba(z, g_q.reshape(1, -1), g_k.reshape(1, -1), batch, seq)
    x1 = _mix_out(hm, ob, z, w_branch_a.astype(BF16), w_branch_b.astype(BF16), w_out.astype(BF16),
                  x2, mod3, seq)
    return _moe(x1, mod3, g_ffn, w_router, b_router, w_up, b_up, w_down, b_down, seq)


def kernel(x, c, w_ada, b_ada, g_mix, w_in, b_igate, b_fgate, g_mlstm_out, g_q, g_k, w_branch_a,
           w_branch_b, w_out, g_ffn, w_router, b_router, w_up, b_up, w_down, b_down):
    batch, seq, d = x.shape
    x2 = x.reshape(batch * seq, d)
    for l in range(w_ada.shape[0]):
        x2 = _layer(x2, c, batch, seq, w_ada[l], b_ada[l], g_mix[l], w_in[l], b_igate[l], b_fgate[l],
                    g_mlstm_out[l], g_q[l], g_k[l], w_branch_a[l], w_branch_b[l], w_out[l], g_ffn[l],
                    w_router[l], b_router[l], w_up[l], b_up[l], w_down[l], b_down[l])
    return x2.reshape(batch, seq, d)
```

```python
import functools

import jax
import jax.numpy as jnp
from jax import lax
from jax.experimental import pallas as pl
from jax.experimental.pallas import tpu as pltpu

F32 = jnp.float32
BF16 = jnp.bfloat16
I32 = jnp.int32

D_MODEL = 2048
MLSTM_HEADS = 8
MLSTM_DV = 128
MLSTM_DK = 64
MLSTM_WIDTH = MLSTM_HEADS * MLSTM_DV
GATE_SOFTCAP = 15.0
MOBA_HEADS = 8
MOBA_HEAD_DIM = 128
MOBA_BLOCK = 256
MOBA_TOPK = 3
MOBA_WIDTH = MOBA_HEADS * MOBA_HEAD_DIM
N_EXPERTS = 32
TOP_K = 4
D_EXPERT = D_MODEL
SWIGLU_ALPHA = 1.702
SWIGLU_LIMIT = 7.0
NORM_EPS = 1e-6
NEG_INF = -1e30

LANES = 128
VMEM_LIMIT = 56 * 1024 * 1024

Z_QA, Z_KA, Z_VA, Z_OA = 0, 512, 1024, 2048
Z_QB, Z_KB, Z_VB = 3072, 4096, 5120
Z_GA, Z_GB = 6144, 8192
Z_WIDTH = 10240

MLSTM_CHUNK = 256

MOE_TM = 256
ROW_CHUNKS = D_MODEL // LANES


def _params(*sem):
    return pltpu.CompilerParams(dimension_semantics=sem, vmem_limit_bytes=VMEM_LIMIT)


def _rms(x, eps=NORM_EPS):
    return x * lax.rsqrt(jnp.mean(x * x, axis=-1, keepdims=True) + eps)


def _sigmoid(x):
    return 1.0 / (1.0 + jnp.exp(-x))


def _ada_kernel(c_ref, w_ref, b_ref, o_ref):
    c = c_ref[...]
    s = (c * _sigmoid(c)).astype(BF16)
    o_ref[...] = jnp.dot(s, w_ref[...].astype(BF16), preferred_element_type=F32) + b_ref[...]


def _ada(c_pad, w_ada, b_ada):
    rows, d = c_pad.shape
    n = w_ada.shape[1]
    tn = 1024
    return pl.pallas_call(
        _ada_kernel,
        grid=(n // tn,),
        in_specs=[pl.BlockSpec((rows, d), lambda j: (0, 0)),
                  pl.BlockSpec((d, tn), lambda j: (0, j)),
                  pl.BlockSpec((1, tn), lambda j: (0, j))],
        out_specs=pl.BlockSpec((rows, tn), lambda j: (0, j)),
        out_shape=jax.ShapeDtypeStruct((rows, n), F32),
        compiler_params=_params("arbitrary"),
        name="ada",
    )(c_pad, w_ada, b_ada.reshape(1, n))


def _in_kernel(x_ref, mod_ref, g_ref, w_ref, wg_ref, z_ref, gates_ref, h_scr):
    @pl.when(pl.program_id(1) == 0)
    def _():
        y = _rms(x_ref[...]) * g_ref[...]
        h = y * (1.0 + mod_ref[1:2, :]) + mod_ref[0:1, :]
        hb = h.astype(BF16)
        h_scr[...] = hb
        gates_ref[...] = jnp.dot(hb, wg_ref[...], preferred_element_type=F32)

    z_ref[...] = jnp.dot(h_scr[...], w_ref[...], preferred_element_type=F32).astype(z_ref.dtype)


def _in_proj(x2, mod3, g_mix, w_main, w_gate, seq):
    t, d = x2.shape
    n = w_main.shape[1]
    tm, tn = 1024, 1024
    per_b = seq // tm
    return pl.pallas_call(
        _in_kernel,
        grid=(t // tm, n // tn),
        in_specs=[pl.BlockSpec((tm, d), lambda i, j: (i, 0)),
                  pl.BlockSpec((None, 8, d), lambda i, j: (i // per_b, 0, 0)),
                  pl.BlockSpec((1, d), lambda i, j: (0, 0)),
                  pl.BlockSpec((d, tn), lambda i, j: (0, j)),
                  pl.BlockSpec((d, LANES), lambda i, j: (0, 0))],
        out_specs=[pl.BlockSpec((tm, tn), lambda i, j: (i, j)),
                   pl.BlockSpec((tm, LANES), lambda i, j: (i, 0))],
        out_shape=[jax.ShapeDtypeStruct((t, n), BF16),
                   jax.ShapeDtypeStruct((t, LANES), F32)],
        scratch_shapes=[pltpu.VMEM((tm, d), BF16)],
        compiler_params=_params("arbitrary", "arbitrary"),
        name="in_proj",
    )(x2, mod3, g_mix, w_main, w_gate)


def _mlstm_kernel(q_ref, k_ref, v_ref, o_ref, gates_ref, gbias_ref, gout_ref, out_ref,
                  c_scr, n_scr, m_scr):
    L = MLSTM_CHUNK
    pair = pl.program_id(1)

    @pl.when(pl.program_id(2) == 0)
    def _():
        c_scr[...] = jnp.zeros_like(c_scr)
        n_scr[...] = jnp.zeros_like(n_scr)
        m_scr[...] = jnp.zeros_like(m_scr)

    pre = gates_ref[...] + gbias_ref[...]
    capped = GATE_SOFTCAP * jnp.tanh(pre / GATE_SOFTCAP)
    log_f = jnp.minimum(capped, 0.0) - jnp.log1p(jnp.exp(-jnp.abs(capped)))
    row = lax.broadcasted_iota(jnp.int32, (L, L), 0)
    col = lax.broadcasted_iota(jnp.int32, (L, L), 1)
    causal = col <= row
    tril = jnp.where(causal, 1.0, 0.0).astype(F32)
    b_all = jnp.dot(tril, log_f, preferred_element_type=F32,
                    precision=lax.Precision.HIGHEST)
    li_t = capped.T
    b_t = b_all.T

    for j in range(2):
        head = 2 * pair + j
        q = q_ref[:, j * MLSTM_DK:(j + 1) * MLSTM_DK] * (MLSTM_DK ** -0.5)
        k = k_ref[:, j * MLSTM_DK:(j + 1) * MLSTM_DK]
        v = v_ref[:, j * MLSTM_DV:(j + 1) * MLSTM_DV]
        lane = lax.broadcasted_iota(jnp.int32, (L, LANES), 1)
        sub = lax.broadcasted_iota(jnp.int32, (LANES, L), 0)
        li_c = jnp.sum(jnp.where(lane == head, capped, 0.0), axis=1, keepdims=True)
        b_c = jnp.sum(jnp.where(lane == head + MLSTM_HEADS, b_all, 0.0), axis=1, keepdims=True)
        li_r = jnp.sum(jnp.where(sub == head, li_t, 0.0), axis=0, keepdims=True)
        b_r = jnp.sum(jnp.where(sub == head + MLSTM_HEADS, b_t, 0.0), axis=0, keepdims=True)
        ct = c_scr[j]
        nvec = n_scr[j]
        m_in = m_scr[j][:, :1]
        g = b_c[L - 1:L, :]

        d = jnp.where(causal, b_c - b_r + li_r, NEG_INF)
        dmax = jnp.max(d, axis=-1, keepdims=True)
        inter = b_c + m_in
        m_j = jnp.maximum(inter, dmax)
        p = jnp.exp(d - m_j)
        s = lax.dot_general(q, k, (((1,), (1,)), ((), ())), preferred_element_type=F32) * p
        s_inter = jnp.exp(inter - m_j)
        qf = q.astype(F32)
        num = (jnp.dot(s.astype(BF16), v, preferred_element_type=F32)
               + s_inter * jnp.dot(q, ct.astype(BF16), preferred_element_type=F32))
        den = (jnp.sum(s, axis=-1, keepdims=True)
               + s_inter * jnp.sum(qf * nvec, axis=-1, keepdims=True))
        h = num / jnp.maximum(jnp.abs(den), jnp.exp(-m_j))

        gout = gout_ref[pl.ds(head, 1), :]
        hn = _rms(h) * gout
        og = o_ref[:, j * MLSTM_DV:(j + 1) * MLSTM_DV].astype(F32)
        out_ref[:, j * MLSTM_DV:(j + 1) * MLSTM_DV] = (hn * _sigmoid(og)).astype(out_ref.dtype)

        a = g - b_c + li_c
        m_loc = jnp.max(a, axis=0, keepdims=True)
        w = jnp.exp(a - m_loc)
        kw = k.astype(F32) * w
        c_loc = lax.dot_general(kw.astype(BF16), v, (((0,), (0,)), ((), ())),
                                preferred_element_type=F32)
        n_loc = jnp.sum(kw, axis=0, keepdims=True)
        m_new = jnp.maximum(g + m_in, m_loc)
        s_prev = jnp.exp(g + m_in - m_new)
        s_loc = jnp.exp(m_loc - m_new)
        c_scr[j] = s_prev * ct + s_loc * c_loc
        n_scr[j] = s_prev * nvec + s_loc * n_loc
        m_scr[j] = jnp.broadcast_to(m_new, (1, LANES))


def _mlstm(z, gates, gbias, g_out, batch, seq):
    t = z.shape[0]
    L = MLSTM_CHUNK
    nc = seq // L
    pairs = MLSTM_HEADS // 2
    rowmap = lambda b, p, c: b * nc + c
    return pl.pallas_call(
        _mlstm_kernel,
        grid=(batch, pairs, nc),
        in_specs=[pl.BlockSpec((L, 128), lambda b, p, c: (rowmap(b, p, c), Z_QA // 128 + p)),
                  pl.BlockSpec((L, 128), lambda b, p, c: (rowmap(b, p, c), Z_KA // 128 + p)),
                  pl.BlockSpec((L, 256), lambda b, p, c: (rowmap(b, p, c), Z_VA // 256 + p)),
                  pl.BlockSpec((L, 256), lambda b, p, c: (rowmap(b, p, c), Z_OA // 256 + p)),
                  pl.BlockSpec((L, LANES), lambda b, p, c: (rowmap(b, p, c), 0)),
                  pl.BlockSpec((1, LANES), lambda b, p, c: (0, 0)),
                  pl.BlockSpec((MLSTM_HEADS, MLSTM_DV), lambda b, p, c: (0, 0))],
        out_specs=pl.BlockSpec((L, 256), lambda b, p, c: (rowmap(b, p, c), p)),
        out_shape=jax.ShapeDtypeStruct((t, MLSTM_WIDTH), BF16),
        scratch_shapes=[pltpu.VMEM((2, MLSTM_DK, MLSTM_DV), F32),
                        pltpu.VMEM((2, 1, MLSTM_DK), F32),
                        pltpu.VMEM((2, 1, LANES), F32)],
        compiler_params=_params("arbitrary", "arbitrary", "arbitrary"),
        name="mlstm",
    )(z, z, z, z, gates, gbias, g_out)


def _moba_kernel(q_ref, k_ref, v_ref, gq_ref, gk_ref, out_ref, kn_scr, kmean_scr):
    BK = MOBA_BLOCK
    nb = k_ref.shape[0] // BK
    qi = pl.program_id(2)
    scale = MOBA_HEAD_DIM ** -0.5

    @pl.when(qi == 0)
    def _():
        kmean_scr[...] = jnp.zeros_like(kmean_scr)
        for n in range(nb):
            kf = k_ref[n * BK:(n + 1) * BK, :].astype(F32)
            kn = _rms(kf) * gk_ref[...]
            kn_scr[n * BK:(n + 1) * BK, :] = kn.astype(BF16)
            kmean_scr[n:n + 1, :] = jnp.mean(kn, axis=0, keepdims=True)

    qn = _rms(q_ref[...].astype(F32)) * gq_ref[...]
    qb = qn.astype(BF16)
    gate = lax.dot_general(qn, kmean_scr[...], (((1,), (1,)), ((), ())),
                           preferred_element_type=F32,
                           precision=lax.Precision.HIGHEST)
    lane = lax.broadcasted_iota(jnp.int32, (BK, LANES), 1)
    cnt = jnp.zeros((BK, LANES), F32)
    for m in range(nb):
        gm = gate[:, m:m + 1]
        beats = (gm > gate) | ((gm == gate) & (m < lane))
        cnt = cnt + jnp.where(beats & (m < qi), 1.0, 0.0)
    sel = jnp.where((cnt < MOBA_TOPK) & (lane < qi), 1.0, 0.0)

    row = lax.broadcasted_iota(jnp.int32, (BK, BK), 0)
    col = lax.broadcasted_iota(jnp.int32, (BK, BK), 1)
    causal = jnp.where(col <= row, 1.0, 0.0)

    def attend(n_blocks):
        nk = n_blocks * BK
        s = lax.dot_general(qb, kn_scr[0:nk, :], (((1,), (1,)), ((), ())),
                            preferred_element_type=F32) * scale
        allowed = jnp.concatenate(
            [jnp.where(qi == n, causal, jnp.broadcast_to(sel[:, n:n + 1], (BK, BK)))
             for n in range(n_blocks)], axis=1)
        s = jnp.where(allowed > 0.0, s, NEG_INF)
        p = jnp.exp(s - jnp.max(s, axis=-1, keepdims=True))
        num = jnp.dot(p.astype(BF16), v_ref[0:nk, :], preferred_element_type=F32)
        out_ref[...] = (num / jnp.sum(p, axis=-1, keepdims=True)).astype(out_ref.dtype)

    half = nb // 2

    @pl.when(qi < half)
    def _():
        attend(half)

    @pl.when(qi >= half)
    def _():
        attend(nb)


def _moba(z, g_q, g_k, batch, seq):
    t = z.shape[0]
    BK = MOBA_BLOCK
    nq = seq // BK
    dh = MOBA_HEAD_DIM
    return pl.pallas_call(
        _moba_kernel,
        grid=(batch, MOBA_HEADS, nq),
        in_specs=[pl.BlockSpec((BK, dh), lambda b, h, i: (b * nq + i, Z_QB // dh + h)),
                  pl.BlockSpec((seq, dh), lambda b, h, i: (b, Z_KB // dh + h)),
                  pl.BlockSpec((seq, dh), lambda b, h, i: (b, Z_VB // dh + h)),
                  pl.BlockSpec((1, dh), lambda b, h, i: (0, 0)),
                  pl.BlockSpec((1, dh), lambda b, h, i: (0, 0))],
        out_specs=pl.BlockSpec((BK, dh), lambda b, h, i: (b * nq + i, h)),
        out_shape=jax.ShapeDtypeStruct((t, MOBA_WIDTH), BF16),
        scratch_shapes=[pltpu.VMEM((seq, dh), BF16),
                        pltpu.VMEM((LANES, dh), F32)],
        compiler_params=_params("arbitrary", "arbitrary", "arbitrary"),
        name="moba",
    )(z, z, z, g_q, g_k)


def _mix_kernel(hm_ref, ob_ref, ga_ref, gb_ref, wa_ref, wb_ref, wo_ref, x_ref, mod_ref,
                out_ref, y_scr):
    @pl.when(pl.program_id(1) == 0)
    def _():
        ya = jnp.dot(hm_ref[...], wa_ref[...], preferred_element_type=F32)
        yb = jnp.dot(ob_ref[...], wb_ref[...], preferred_element_type=F32)
        y = (_sigmoid(ga_ref[...].astype(F32)) * ya + _sigmoid(gb_ref[...].astype(F32)) * yb)
        y_scr[...] = y.astype(BF16)

    proj = jnp.dot(y_scr[...], wo_ref[...], preferred_element_type=F32)
    out_ref[...] = x_ref[...] + mod_ref[2:3, :] * proj


def _mix_out(hm, ob, z, wa, wb, wo, x2, mod3, seq):
    t, d = x2.shape
    tm, tn = 512, 512
    per_b = seq // tm
    return pl.pallas_call(
        _mix_kernel,
        grid=(t // tm, d // tn),
        in_specs=[pl.BlockSpec((tm, MLSTM_WIDTH), lambda i, j: (i, 0)),
                  pl.BlockSpec((tm, MOBA_WIDTH), lambda i, j: (i, 0)),
                  pl.BlockSpec((tm, d), lambda i, j: (i, Z_GA // d)),
                  pl.BlockSpec((tm, d), lambda i, j: (i, Z_GB // d)),
                  pl.BlockSpec((MLSTM_WIDTH, d), lambda i, j: (0, 0)),
                  pl.BlockSpec((MOBA_WIDTH, d), lambda i, j: (0, 0)),
                  pl.BlockSpec((d, tn), lambda i, j: (0, j)),
                  pl.BlockSpec((tm, tn), lambda i, j: (i, j)),
                  pl.BlockSpec((None, 8, tn), lambda i, j: (i // per_b, 0, j))],
        out_specs=pl.BlockSpec((tm, tn), lambda i, j: (i, j)),
        out_shape=jax.ShapeDtypeStruct((t, d), F32),
        scratch_shapes=[pltpu.VMEM((tm, d), BF16)],
        compiler_params=_params("arbitrary", "arbitrary"),
        name="mix_out",
    )(hm, ob, z, z, wa, wb, wo, x2, mod3)


def _route_kernel(x_ref, mod_ref, g_ref, wr_ref, br_ref, t_ref, ri_ref, rw_ref, cnt_ref, cnt_scr):
    i = pl.program_id(0)

    @pl.when(i == 0)
    def _():
        cnt_scr[...] = jnp.zeros_like(cnt_scr)

    y = _rms(x_ref[...]) * g_ref[...]
    tt = y * (1.0 + mod_ref[4:5, :]) + mod_ref[3:4, :]
    tm = tt.shape[0]
    for s in range(ROW_CHUNKS):
        t_ref[pl.ds(s, tm, stride=ROW_CHUNKS), :] = tt[:, s * LANES:(s + 1) * LANES]

    logits = jnp.dot(tt, wr_ref[...], preferred_element_type=F32,
                     precision=lax.Precision.HIGHEST) + br_ref[...]
    lane = lax.broadcasted_iota(I32, (tm, LANES), 1)
    work = logits
    picks = []
    for _ in range(TOP_K):
        mx = jnp.max(work, axis=-1, keepdims=True)
        first = jnp.min(jnp.where(work == mx, lane, LANES), axis=-1, keepdims=True)
        picks.append((first, mx))
        work = jnp.where(lane == first, -jnp.inf, work)
    top = picks[0][1]
    exps = [jnp.exp(v - top) for _, v in picks]
    denom = exps[0] + exps[1] + exps[2] + exps[3]

    sel = jnp.zeros((tm, LANES), F32)
    for first, _ in picks:
        sel = sel + jnp.where(lane == first, 1.0, 0.0)
    row = lax.broadcasted_iota(I32, (tm, tm), 0)
    col = lax.broadcasted_iota(I32, (tm, tm), 1)
    strict = jnp.where(col < row, 1.0, 0.0).astype(BF16)
    before = jnp.dot(strict, sel.astype(BF16), preferred_element_type=F32) + cnt_scr[...]

    ri = jnp.zeros((tm, LANES), I32)
    rw = jnp.zeros((tm, LANES), F32)
    for kk, (first, _) in enumerate(picks):
        rank = jnp.sum(jnp.where(lane == first, before, 0.0), axis=-1, keepdims=True).astype(I32)
        ri = jnp.where(lane == kk, first, ri)
        ri = jnp.where(lane == TOP_K + kk, rank, ri)
        rw = jnp.where(lane == kk, exps[kk] / denom, rw)
    ri_ref[...] = ri
    rw_ref[...] = rw
    cnt_scr[...] = cnt_scr[...] + jnp.sum(sel, axis=0, keepdims=True)
    cnt_ref[...] = jnp.broadcast_to(cnt_scr[...], cnt_ref.shape)


def _route(x1, mod3, g_ffn, w_router_pad, b_router_pad, seq):
    t, d = x1.shape
    tm = 512
    per_b = seq // tm
    return pl.pallas_call(
        _route_kernel,
        grid=(t // tm,),
        in_specs=[pl.BlockSpec((tm, d), lambda i: (i, 0)),
                  pl.BlockSpec((None, 8, d), lambda i: (i // per_b, 0, 0)),
                  pl.BlockSpec((1, d), lambda i: (0, 0)),
                  pl.BlockSpec((d, LANES), lambda i: (0, 0)),
                  pl.BlockSpec((1, LANES), lambda i: (0, 0))],
        out_specs=[pl.BlockSpec((tm * ROW_CHUNKS, LANES), lambda i: (i, 0)),
                   pl.BlockSpec((tm, LANES), lambda i: (i, 0)),
                   pl.BlockSpec((tm, LANES), lambda i: (i, 0)),
                   pl.BlockSpec((8, LANES), lambda i: (0, 0))],
        out_shape=[jax.ShapeDtypeStruct((t * ROW_CHUNKS, LANES), F32),
                   jax.ShapeDtypeStruct((t, LANES), I32),
                   jax.ShapeDtypeStruct((t, LANES), F32),
                   jax.ShapeDtypeStruct((8, LANES), F32)],
        scratch_shapes=[pltpu.VMEM((1, LANES), F32)],
        compiler_params=_params("arbitrary"),
        name="route",
    )(x1, mod3, g_ffn, w_router_pad, b_router_pad)


def _inv_kernel(pos_ref, src_ref):
    n_rows = src_ref.shape[0]
    n_pairs = pos_ref.shape[0]

    def zero(i, c):
        src_ref[i] = 0
        return c

    lax.fori_loop(0, n_rows, zero, 0, unroll=8)

    def put(i, c):
        src_ref[pos_ref[i]] = lax.shift_right_logical(i, 2)
        return c

    lax.fori_loop(0, n_pairs, put, 0, unroll=8)


def _inverse(pos_flat, n_rows):
    return pl.pallas_call(
        _inv_kernel,
        in_specs=[pl.BlockSpec(memory_space=pltpu.SMEM)],
        out_specs=pl.BlockSpec(memory_space=pltpu.SMEM),
        out_shape=jax.ShapeDtypeStruct((n_rows,), I32),
        name="inverse",
    )(pos_flat)


def _row_copy(src_hbm, tok, buf, slot, r, sem):
    return pltpu.make_async_copy(src_hbm.at[pl.ds(tok * ROW_CHUNKS, ROW_CHUNKS), :],
                                 buf.at[slot, pl.ds(r * ROW_CHUNKS, ROW_CHUNKS), :],
                                 sem.at[slot])


def _dispatch_kernel(src_ref, nt_ref, t_hbm, out_ref, buf, sem):
    i = pl.program_id(0)
    nt = nt_ref[0]
    tm = out_ref.shape[0]

    def issue(tile, slot):
        def body(h, c):
            for par in range(2):
                r = 2 * h + par
                tok = src_ref[tile * tm + r]
                _row_copy(t_hbm, tok, buf, slot, r, sem).start(priority=par)
            return c
        lax.fori_loop(0, tm // 2, body, 0, unroll=4)

    @pl.when(i == 0)
    def _():
        issue(0, 0)

    @pl.when(i + 1 < nt)
    def _():
        issue(i + 1, (i + 1) % 2)

    @pl.when(i < nt)
    def _():
        slot = i % 2
        pltpu.make_async_copy(t_hbm.at[pl.ds(0, tm * ROW_CHUNKS), :], buf.at[slot], sem.at[slot]).wait()
        for s in range(ROW_CHUNKS):
            chunk = buf[slot, pl.ds(s, tm, stride=ROW_CHUNKS), :]
            out_ref[:, s * LANES:(s + 1) * LANES] = chunk.astype(out_ref.dtype)

    @pl.when(i >= nt)
    def _():
        out_ref[...] = jnp.zeros_like(out_ref)


def _dispatch(src, nt, t_rows, n_rows):
    tm = MOE_TM
    grid_spec = pltpu.PrefetchScalarGridSpec(
        num_scalar_prefetch=2,
        grid=(n_rows // tm,),
        in_specs=[pl.BlockSpec(memory_space=pl.ANY)],
        out_specs=pl.BlockSpec((tm, D_MODEL), lambda i, src, nt: (i, 0)),
        scratch_shapes=[pltpu.VMEM((2, tm * ROW_CHUNKS, LANES), F32),
                        pltpu.SemaphoreType.DMA((2,))],
    )
    return pl.pallas_call(
        _dispatch_kernel,
        grid_spec=grid_spec,
        out_shape=jax.ShapeDtypeStruct((n_rows, D_MODEL), BF16),
        compiler_params=_params("arbitrary"),
        name="dispatch",
    )(src, nt, t_rows)


def _gmm1_kernel(te_ref, nx_ref, nt_ref, x_ref, w_hbm, bg_ref, bl_ref, act_ref,
                 wf_buf, wg_scr, wl_scr, sem, *, nj):
    j = pl.program_id(0)
    i = pl.program_id(1)
    tj = act_ref.shape[1]

    def copies(e, jj):
        return [pltpu.make_async_copy(w_hbm.at[e, :, (h * nj + jj) * tj:(h * nj + jj + 1) * tj],
                                      wf_buf.at[h], sem.at[h]) for h in range(2)]

    def start(e, jdyn):
        for jj in range(nj):
            @pl.when(jdyn == jj)
            def _(jj=jj):
                for cp in copies(e, jj):
                    cp.start()

    @pl.when(i < nt_ref[0])
    def _():
        e = te_ref[i]
        new_group = (i == 0) | (e != te_ref[jnp.maximum(i - 1, 0)])

        @pl.when((i == 0) & (j == 0))
        def _():
            start(e, j)

        @pl.when(new_group)
        def _():
            for cp in copies(e, 0):
                cp.wait()
            wg_scr[...] = wf_buf[0].astype(BF16)
            wl_scr[...] = wf_buf[1].astype(BF16)
            nxt = nx_ref[i]

            @pl.when(nxt >= 0)
            def _():
                start(nxt, j)

            @pl.when((nxt < 0) & (j + 1 < nj))
            def _():
                start(te_ref[0], j + 1)

        x = x_ref[...]
        glu = jnp.dot(x, wg_scr[...], preferred_element_type=F32) + bg_ref[...]
        lin = jnp.dot(x, wl_scr[...], preferred_element_type=F32) + bl_ref[...]
        glu = jnp.minimum(glu, SWIGLU_LIMIT)
        lin = jnp.clip(lin, -SWIGLU_LIMIT, SWIGLU_LIMIT)
        act_ref[...] = (glu * _sigmoid(SWIGLU_ALPHA * glu) * (lin + 1.0)).astype(act_ref.dtype)

    @pl.when(i >= nt_ref[0])
    def _():
        act_ref[...] = jnp.zeros_like(act_ref)


def _gmm1(te, nx, nt, xs, w_up, b_up3):
    n_rows, d = xs.shape
    de = w_up.shape[2] // 2
    tm, tj = MOE_TM, 1024
    nj = de // tj
    row = lambda j, i, te, nx, nt: jnp.maximum(jnp.minimum(i, nt[0] - 1), 0)
    exp = lambda j, i, te, nx, nt: te[row(j, i, te, nx, nt)]
    grid_spec = pltpu.PrefetchScalarGridSpec(
        num_scalar_prefetch=3,
        grid=(nj, n_rows // tm),
        in_specs=[pl.BlockSpec((tm, d), lambda j, i, te, nx, nt: (row(j, i, te, nx, nt), 0)),
                  pl.BlockSpec(memory_space=pl.ANY),
                  pl.BlockSpec((None, 1, tj), lambda j, i, te, nx, nt: (exp(j, i, te, nx, nt), 0, j)),
                  pl.BlockSpec((None, 1, tj), lambda j, i, te, nx, nt: (exp(j, i, te, nx, nt), 0, nj + j))],
        out_specs=pl.BlockSpec((tm, tj), lambda j, i, te, nx, nt: (i, j)),
        scratch_shapes=[pltpu.VMEM((2, d, tj), F32), pltpu.VMEM((d, tj), BF16), pltpu.VMEM((d, tj), BF16),
                        pltpu.SemaphoreType.DMA((2,))],
    )
    return pl.pallas_call(
        functools.partial(_gmm1_kernel, nj=nj),
        grid_spec=grid_spec,
        out_shape=jax.ShapeDtypeStruct((n_rows, de), BF16),
        compiler_params=_params("arbitrary", "arbitrary"),
        name="gmm1",
    )(te, nx, nt, xs, w_up, b_up3, b_up3)


def _gmm2_kernel(te_ref, nx_ref, nt_ref, a_ref, wd_hbm, bd_ref, y_ref, wf_buf, wd_scr, sem):
    i = pl.program_id(0)

    def fetch(e):
        return pltpu.make_async_copy(wd_hbm.at[e], wf_buf, sem.at[0])

    @pl.when(i < nt_ref[0])
    def _():
        e = te_ref[i]
        new_group = (i == 0) | (e != te_ref[jnp.maximum(i - 1, 0)])

        @pl.when(i == 0)
        def _():
            fetch(e).start()

        @pl.when(new_group)
        def _():
            fetch(e).wait()
            wd_scr[...] = wf_buf[...].astype(BF16)
            nxt = nx_ref[i]

            @pl.when(nxt >= 0)
            def _():
                fetch(nxt).start()

        y = jnp.dot(a_ref[...], wd_scr[...], preferred_element_type=F32) + bd_ref[...]
        tm = y.shape[0]
        for s in range(ROW_CHUNKS):
            y_ref[pl.ds(s, tm, stride=ROW_CHUNKS), :] = y[:, s * LANES:(s + 1) * LANES]

    @pl.when(i >= nt_ref[0])
    def _():
        y_ref[...] = jnp.zeros_like(y_ref)


def _gmm2(te, nx, nt, act, w_down, b_down3):
    n_rows, de = act.shape
    d = w_down.shape[2]
    tm = MOE_TM
    row = lambda i, te, nx, nt: jnp.maximum(jnp.minimum(i, nt[0] - 1), 0)
    exp = lambda i, te, nx, nt: te[row(i, te, nx, nt)]
    grid_spec = pltpu.PrefetchScalarGridSpec(
        num_scalar_prefetch=3,
        grid=(n_rows // tm,),
        in_specs=[pl.BlockSpec((tm, de), lambda i, te, nx, nt: (row(i, te, nx, nt), 0)),
                  pl.BlockSpec(memory_space=pl.ANY),
                  pl.BlockSpec((None, 1, d), lambda i, te, nx, nt: (exp(i, te, nx, nt), 0, 0))],
        out_specs=pl.BlockSpec((tm * ROW_CHUNKS, LANES), lambda i, te, nx, nt: (i, 0)),
        scratch_shapes=[pltpu.VMEM((de, d), F32), pltpu.VMEM((de, d), BF16),
                        pltpu.SemaphoreType.DMA((1,))],
    )
    return pl.pallas_call(
        _gmm2_kernel,
        grid_spec=grid_spec,
        out_shape=jax.ShapeDtypeStruct((n_rows * ROW_CHUNKS, LANES), F32),
        compiler_params=_params("arbitrary"),
        name="gmm2",
    )(te, nx, nt, act, w_down, b_down3)


def _combine_kernel(pos_ref, y_hbm, rw_ref, x_ref, mod_ref, out_ref, buf, sem):
    i = pl.program_id(0)
    n = pl.num_programs(0)
    tm = out_ref.shape[0]

    def issue(tile, slot):
        def body(r, c):
            for kk in range(TOP_K):
                p = pos_ref[(tile * tm + r) * TOP_K + kk]
                pltpu.make_async_copy(y_hbm.at[pl.ds(p * ROW_CHUNKS, ROW_CHUNKS), :],
                                      buf.at[slot, kk, pl.ds(r * ROW_CHUNKS, ROW_CHUNKS), :],
                                      sem.at[slot]).start(priority=kk % 2)
            return c
        lax.fori_loop(0, tm, body, 0, unroll=4)

    @pl.when(i == 0)
    def _():
        issue(0, 0)

    @pl.when(i + 1 < n)
    def _():
        issue(i + 1, (i + 1) % 2)

    slot = i % 2
    for kk in range(TOP_K):
        pltpu.make_async_copy(y_hbm.at[pl.ds(0, tm * ROW_CHUNKS), :], buf.at[slot, kk],
                              sem.at[slot]).wait()
    rw = rw_ref[...]
    gate = mod_ref[5:6, :]
    for s in range(ROW_CHUNKS):
        acc = None
        for kk in range(TOP_K):
            chunk = buf[slot, kk, pl.ds(s, tm, stride=ROW_CHUNKS), :]
            term = rw[:, kk:kk + 1] * chunk
            acc = term if acc is None else acc + term
        sl = slice(s * LANES, (s + 1) * LANES)
        out_ref[:, sl] = x_ref[:, sl] + gate[:, sl] * acc


def _combine(pos_flat, y_rows, rw, x1, mod3, seq):
    t, d = x1.shape
    tm = 256
    per_b = seq // tm
    grid_spec = pltpu.PrefetchScalarGridSpec(
        num_scalar_prefetch=1,
        grid=(t // tm,),
        in_specs=[pl.BlockSpec(memory_space=pl.ANY),
                  pl.BlockSpec((tm, LANES), lambda i, pos: (i, 0)),
                  pl.BlockSpec((tm, d), lambda i, pos: (i, 0)),
                  pl.BlockSpec((None, 8, d), lambda i, pos: (i // per_b, 0, 0))],
        out_specs=pl.BlockSpec((tm, d), lambda i, pos: (i, 0)),
        scratch_shapes=[pltpu.VMEM((2, TOP_K, tm * ROW_CHUNKS, LANES), F32),
                        pltpu.SemaphoreType.DMA((2,))],
    )
    return pl.pallas_call(
        _combine_kernel,
        grid_spec=grid_spec,
        out_shape=jax.ShapeDtypeStruct((t, d), F32),
        compiler_params=_params("arbitrary"),
        name="combine",
    )(pos_flat, y_rows, rw, x1, mod3)


def _moe(x1, mod3, g_ffn, w_router, b_router, w_up, b_up, w_down, b_down, seq):
    t, d = x1.shape
    ne = w_up.shape[0]
    tm = MOE_TM
    n_rows = t * TOP_K + ne * tm
    n_tiles = n_rows // tm

    wr = jnp.pad(w_router, ((0, 0), (0, LANES - ne)))
    br = jnp.concatenate([b_router, jnp.full((LANES - ne,), NEG_INF, F32)]).reshape(1, LANES)
    t_rows, ri, rw, cnt = _route(x1, mod3, g_ffn.reshape(1, d), wr, br, seq)

    counts = cnt[0, :ne].astype(I32)
    tiles_per = (counts + tm - 1) // tm
    tile_end = jnp.cumsum(tiles_per)
    offs = (tile_end - tiles_per) * tm
    eidx = ri[:, :TOP_K]
    rank = ri[:, TOP_K:2 * TOP_K]
    onehot = eidx[..., None] == jnp.arange(ne, dtype=I32)
    pos = (jnp.sum(jnp.where(onehot, offs, 0), axis=-1) + rank).reshape(-1)
    nt = tile_end[-1:].astype(I32)
    te = jnp.minimum(jnp.sum(jnp.arange(n_tiles, dtype=I32)[:, None] >= tile_end[None, :], axis=1),
                     ne - 1).astype(I32)

    next_tile = jnp.sum(jnp.where(te[:, None] == jnp.arange(ne, dtype=I32), tile_end[None, :], 0), axis=1)
    nx = jnp.where(next_tile < nt[0], te[jnp.minimum(next_tile, n_tiles - 1)], -1).astype(I32)

    src = _inverse(pos, n_rows)
    xs = _dispatch(src, nt, t_rows, n_rows)
    act = _gmm1(te, nx, nt, xs, w_up, b_up.reshape(ne, 1, -1))
    y_rows = _gmm2(te, nx, nt, act, w_down, b_down.reshape(ne, 1, d))
    return _combine(pos, y_rows, rw, x1, mod3, seq)


def _layer(x2, c, batch, seq, w_ada, b_ada, g_mix, w_in, b_igate, b_fgate, g_mlstm_out, g_q, g_k,
           w_branch_a, w_branch_b, w_out, g_ffn, w_router, b_router, w_up, b_up, w_down, b_down):
    d = D_MODEL
    c_pad = jnp.zeros((8, d), F32).at[:batch].set(c)
    mod = _ada(c_pad, w_ada, b_ada)
    mod3 = jnp.pad(mod[:batch].reshape(batch, 6, d), ((0, 0), (0, 2), (0, 0)))

    i0 = Z_QB
    w_main = jnp.concatenate([w_in[:, :i0], w_in[:, i0 + 2 * MLSTM_HEADS:]], axis=1).astype(BF16)
    w_gate = jnp.pad(w_in[:, i0:i0 + 2 * MLSTM_HEADS],
                     ((0, 0), (0, LANES - 2 * MLSTM_HEADS))).astype(BF16)
    gbias = jnp.pad(jnp.concatenate([b_igate, b_fgate]), (0, LANES - 2 * MLSTM_HEADS)).reshape(1, LANES)

    z, gates = _in_proj(x2, mod3, g_mix.reshape(1, d), w_main, w_gate, seq)
    hm = _mlstm(z, gates, gbias, g_mlstm_out, batch, seq)
    ob = _moba(z, g_q.reshape(1, -1), g_k.reshape(1, -1), batch, seq)
    x1 = _mix_out(hm, ob, z, w_branch_a.astype(BF16), w_branch_b.astype(BF16), w_out.astype(BF16),
                  x2, mod3, seq)
    return _moe(x1, mod3, g_ffn, w_router, b_router, w_up, b_up, w_down, b_down, seq)


def kernel(x, c, w_ada, b_ada, g_mix, w_in, b_igate, b_fgate, g_mlstm_out, g_q, g_k, w_branch_a,
           w_branch_b, w_out, g_ffn, w_router, b_router, w_up, b_up, w_down, b_down):
    batch, seq, d = x.shape
    x2 = x.reshape(batch * seq, d)
    for l in range(w_ada.shape[0]):
        x2 = _layer(x2, c, batch, seq, w_ada[l], b_ada[l], g_mix[l], w_in[l], b_igate[l], b_fgate[l],
                    g_mlstm_out[l], g_q[l], g_k[l], w_branch_a[l], w_branch_b[l], w_out[l], g_ffn[l],
                    w_router[l], b_router[l], w_up[l], b_up[l], w_down[l], b_down[l])
    return x2.reshape(batch, seq, d)
```

```python
import functools

import jax
import jax.numpy as jnp
from jax import lax
from jax.experimental import pallas as pl
from jax.experimental.pallas import tpu as pltpu

F32 = jnp.float32
BF16 = jnp.bfloat16
I32 = jnp.int32

D_MODEL = 2048
MLSTM_HEADS = 8
MLSTM_DV = 128
MLSTM_DK = 64
MLSTM_WIDTH = MLSTM_HEADS * MLSTM_DV
GATE_SOFTCAP = 15.0
MOBA_HEADS = 8
MOBA_HEAD_DIM = 128
MOBA_BLOCK = 256
MOBA_TOPK = 3
MOBA_WIDTH = MOBA_HEADS * MOBA_HEAD_DIM
N_EXPERTS = 32
TOP_K = 4
D_EXPERT = D_MODEL
SWIGLU_ALPHA = 1.702
SWIGLU_LIMIT = 7.0
NORM_EPS = 1e-6
NEG_INF = -1e30

LANES = 128
VMEM_LIMIT = 56 * 1024 * 1024

Z_QA, Z_KA, Z_VA, Z_OA = 0, 512, 1024, 2048
Z_QB, Z_KB, Z_VB = 3072, 4096, 5120
Z_GA, Z_GB = 6144, 8192
Z_WIDTH = 10240

MLSTM_CHUNK = 256

MOE_TM = 256
ROW_CHUNKS = D_MODEL // LANES


def _params(*sem):
    return pltpu.CompilerParams(dimension_semantics=sem, vmem_limit_bytes=VMEM_LIMIT)


def _rms(x, eps=NORM_EPS):
    return x * lax.rsqrt(jnp.mean(x * x, axis=-1, keepdims=True) + eps)


def _sigmoid(x):
    return 1.0 / (1.0 + jnp.exp(-x))


def _ada_kernel(c_ref, w_ref, b_ref, o_ref):
    c = c_ref[...]
    s = (c * _sigmoid(c)).astype(BF16)
    o_ref[...] = jnp.dot(s, w_ref[...].astype(BF16), preferred_element_type=F32) + b_ref[...]


def _ada(c_pad, w_ada, b_ada):
    rows, d = c_pad.shape
    n = w_ada.shape[1]
    tn = 1024
    return pl.pallas_call(
        _ada_kernel,
        grid=(n // tn,),
        in_specs=[pl.BlockSpec((rows, d), lambda j: (0, 0)),
                  pl.BlockSpec((d, tn), lambda j: (0, j)),
                  pl.BlockSpec((1, tn), lambda j: (0, j))],
        out_specs=pl.BlockSpec((rows, tn), lambda j: (0, j)),
        out_shape=jax.ShapeDtypeStruct((rows, n), F32),
        compiler_params=_params("arbitrary"),
        name="ada",
    )(c_pad, w_ada, b_ada.reshape(1, n))


def _in_kernel(x_ref, mod_ref, g_ref, w_ref, wg_ref, z_ref, gates_ref, h_scr):
    @pl.when(pl.program_id(1) == 0)
    def _():
        y = _rms(x_ref[...]) * g_ref[...]
        h = y * (1.0 + mod_ref[1:2, :]) + mod_ref[0:1, :]
        hb = h.astype(BF16)
        h_scr[...] = hb
        gates_ref[...] = jnp.dot(hb, wg_ref[...], preferred_element_type=F32)

    z_ref[...] = jnp.dot(h_scr[...], w_ref[...], preferred_element_type=F32).astype(z_ref.dtype)


def _in_proj(x2, mod3, g_mix, w_main, w_gate, seq):
    t, d = x2.shape
    n = w_main.shape[1]
    tm, tn = 1024, 1024
    per_b = seq // tm
    return pl.pallas_call(
        _in_kernel,
        grid=(t // tm, n // tn),
        in_specs=[pl.BlockSpec((tm, d), lambda i, j: (i, 0)),
                  pl.BlockSpec((None, 8, d), lambda i, j: (i // per_b, 0, 0)),
                  pl.BlockSpec((1, d), lambda i, j: (0, 0)),
                  pl.BlockSpec((d, tn), lambda i, j: (0, j)),
                  pl.BlockSpec((d, LANES), lambda i, j: (0, 0))],
        out_specs=[pl.BlockSpec((tm, tn), lambda i, j: (i, j)),
                   pl.BlockSpec((tm, LANES), lambda i, j: (i, 0))],
        out_shape=[jax.ShapeDtypeStruct((t, n), BF16),
                   jax.ShapeDtypeStruct((t, LANES), F32)],
        scratch_shapes=[pltpu.VMEM((tm, d), BF16)],
        compiler_params=_params("arbitrary", "arbitrary"),
        name="in_proj",
    )(x2, mod3, g_mix, w_main, w_gate)


def _mlstm_kernel(q_ref, k_ref, v_ref, o_ref, gates_ref, gbias_ref, gout_ref, out_ref,
                  c_scr, n_scr, m_scr):
    L = MLSTM_CHUNK
    pair = pl.program_id(1)

    @pl.when(pl.program_id(2) == 0)
    def _():
        c_scr[...] = jnp.zeros_like(c_scr)
        n_scr[...] = jnp.zeros_like(n_scr)
        m_scr[...] = jnp.zeros_like(m_scr)

    pre = gates_ref[...] + gbias_ref[...]
    capped = GATE_SOFTCAP * jnp.tanh(pre / GATE_SOFTCAP)
    log_f = jnp.minimum(capped, 0.0) - jnp.log1p(jnp.exp(-jnp.abs(capped)))
    row = lax.broadcasted_iota(jnp.int32, (L, L), 0)
    col = lax.broadcasted_iota(jnp.int32, (L, L), 1)
    causal = col <= row
    tril = jnp.where(causal, 1.0, 0.0).astype(F32)
    b_all = jnp.dot(tril, log_f, preferred_element_type=F32,
                    precision=lax.Precision.HIGHEST)
    li_t = capped.T
    b_t = b_all.T

    for j in range(2):
        head = 2 * pair + j
        q = q_ref[:, j * MLSTM_DK:(j + 1) * MLSTM_DK] * (MLSTM_DK ** -0.5)
        k = k_ref[:, j * MLSTM_DK:(j + 1) * MLSTM_DK]
        v = v_ref[:, j * MLSTM_DV:(j + 1) * MLSTM_DV]
        lane = lax.broadcasted_iota(jnp.int32, (L, LANES), 1)
        sub = lax.broadcasted_iota(jnp.int32, (LANES, L), 0)
        li_c = jnp.sum(jnp.where(lane == head, capped, 0.0), axis=1, keepdims=True)
        b_c = jnp.sum(jnp.where(lane == head + MLSTM_HEADS, b_all, 0.0), axis=1, keepdims=True)
        li_r = jnp.sum(jnp.where(sub == head, li_t, 0.0), axis=0, keepdims=True)
        b_r = jnp.sum(jnp.where(sub == head + MLSTM_HEADS, b_t, 0.0), axis=0, keepdims=True)
        ct = c_scr[j]
        nvec = n_scr[j]
        m_in = m_scr[j][:, :1]
        g = b_c[L - 1:L, :]

        d = jnp.where(causal, b_c - b_r + li_r, NEG_INF)
        dmax = jnp.max(d, axis=-1, keepdims=True)
        inter = b_c + m_in
        m_j = jnp.maximum(inter, dmax)
        p = jnp.exp(d - m_j)
        s = lax.dot_general(q, k, (((1,), (1,)), ((), ())), preferred_element_type=F32) * p
        s_inter = jnp.exp(inter - m_j)
        qf = q.astype(F32)
        num = (jnp.dot(s.astype(BF16), v, preferred_element_type=F32)
               + s_inter * jnp.dot(q, ct.astype(BF16), preferred_element_type=F32))
        den = (jnp.sum(s, axis=-1, keepdims=True)
               + s_inter * jnp.sum(qf * nvec, axis=-1, keepdims=True))
        h = num / jnp.maximum(jnp.abs(den), jnp.exp(-m_j))

        gout = gout_ref[pl.ds(head, 1), :]
        hn = _rms(h) * gout
        og = o_ref[:, j * MLSTM_DV:(j + 1) * MLSTM_DV].astype(F32)
        out_ref[:, j * MLSTM_DV:(j + 1) * MLSTM_DV] = (hn * _sigmoid(og)).astype(out_ref.dtype)

        a = g - b_c + li_c
        m_loc = jnp.max(a, axis=0, keepdims=True)
        w = jnp.exp(a - m_loc)
        kw = k.astype(F32) * w
        c_loc = lax.dot_general(kw.astype(BF16), v, (((0,), (0,)), ((), ())),
                                preferred_element_type=F32)
        n_loc = jnp.sum(kw, axis=0, keepdims=True)
        m_new = jnp.maximum(g + m_in, m_loc)
        s_prev = jnp.exp(g + m_in - m_new)
        s_loc = jnp.exp(m_loc - m_new)
        c_scr[j] = s_prev * ct + s_loc * c_loc
        n_scr[j] = s_prev * nvec + s_loc * n_loc
        m_scr[j] = jnp.broadcast_to(m_new, (1, LANES))


def _mlstm(z, gates, gbias, g_out, batch, seq):
    t = z.shape[0]
    L = MLSTM_CHUNK
    nc = seq // L
    pairs = MLSTM_HEADS // 2
    rowmap = lambda b, p, c: b * nc + c
    return pl.pallas_call(
        _mlstm_kernel,
        grid=(batch, pairs, nc),
        in_specs=[pl.BlockSpec((L, 128), lambda b, p, c: (rowmap(b, p, c), Z_QA // 128 + p)),
                  pl.BlockSpec((L, 128), lambda b, p, c: (rowmap(b, p, c), Z_KA // 128 + p)),
                  pl.BlockSpec((L, 256), lambda b, p, c: (rowmap(b, p, c), Z_VA // 256 + p)),
                  pl.BlockSpec((L, 256), lambda b, p, c: (rowmap(b, p, c), Z_OA // 256 + p)),
                  pl.BlockSpec((L, LANES), lambda b, p, c: (rowmap(b, p, c), 0)),
                  pl.BlockSpec((1, LANES), lambda b, p, c: (0, 0)),
                  pl.BlockSpec((MLSTM_HEADS, MLSTM_DV), lambda b, p, c: (0, 0))],
        out_specs=pl.BlockSpec((L, 256), lambda b, p, c: (rowmap(b, p, c), p)),
        out_shape=jax.ShapeDtypeStruct((t, MLSTM_WIDTH), BF16),
        scratch_shapes=[pltpu.VMEM((2, MLSTM_DK, MLSTM_DV), F32),
                        pltpu.VMEM((2, 1, MLSTM_DK), F32),
                        pltpu.VMEM((2, 1, LANES), F32)],
        compiler_params=_params("arbitrary", "arbitrary", "arbitrary"),
        name="mlstm",
    )(z, z, z, z, gates, gbias, g_out)


def _moba_kernel(q_ref, k_ref, v_ref, gq_ref, gk_ref, out_ref, kn_scr, kmean_scr):
    BK = MOBA_BLOCK
    nb = k_ref.shape[0] // BK
    qi = pl.program_id(2)
    scale = MOBA_HEAD_DIM ** -0.5

    @pl.when(qi == 0)
    def _():
        kmean_scr[...] = jnp.zeros_like(kmean_scr)
        for n in range(nb):
            kf = k_ref[n * BK:(n + 1) * BK, :].astype(F32)
            kn = _rms(kf) * gk_ref[...]
            kn_scr[n * BK:(n + 1) * BK, :] = kn.astype(BF16)
            kmean_scr[n:n + 1, :] = jnp.mean(kn, axis=0, keepdims=True)

    qn = _rms(q_ref[...].astype(F32)) * gq_ref[...]
    qb = qn.astype(BF16)
    gate = lax.dot_general(qn, kmean_scr[...], (((1,), (1,)), ((), ())),
                           preferred_element_type=F32,
                           precision=lax.Precision.HIGHEST)
    lane = lax.broadcasted_iota(jnp.int32, (BK, LANES), 1)
    cnt = jnp.zeros((BK, LANES), F32)
    for m in range(nb):
        gm = gate[:, m:m + 1]
        beats = (gm > gate) | ((gm == gate) & (m < lane))
        cnt = cnt + jnp.where(beats & (m < qi), 1.0, 0.0)
    sel = jnp.where((cnt < MOBA_TOPK) & (lane < qi), 1.0, 0.0)

    row = lax.broadcasted_iota(jnp.int32, (BK, BK), 0)
    col = lax.broadcasted_iota(jnp.int32, (BK, BK), 1)
    causal = jnp.where(col <= row, 1.0, 0.0)

    def attend(n_blocks):
        nk = n_blocks * BK
        s = lax.dot_general(qb, kn_scr[0:nk, :], (((1,), (1,)), ((), ())),
                            preferred_element_type=F32) * scale
        allowed = jnp.concatenate(
            [jnp.where(qi == n, causal, jnp.broadcast_to(sel[:, n:n + 1], (BK, BK)))
             for n in range(n_blocks)], axis=1)
        s = jnp.where(allowed > 0.0, s, NEG_INF)
        p = jnp.exp(s - jnp.max(s, axis=-1, keepdims=True))
        num = jnp.dot(p.astype(BF16), v_ref[0:nk, :], preferred_element_type=F32)
        out_ref[...] = (num / jnp.sum(p, axis=-1, keepdims=True)).astype(out_ref.dtype)

    step = 2
    for hi in range(step, nb + 1, step):
        @pl.when((qi >= hi - step) & (qi < hi))
        def _(hi=hi):
            attend(hi)


def _moba(z, g_q, g_k, batch, seq):
    t = z.shape[0]
    BK = MOBA_BLOCK
    nq = seq // BK
    dh = MOBA_HEAD_DIM
    return pl.pallas_call(
        _moba_kernel,
        grid=(batch, MOBA_HEADS, nq),
        in_specs=[pl.BlockSpec((BK, dh), lambda b, h, i: (b * nq + i, Z_QB // dh + h)),
                  pl.BlockSpec((seq, dh), lambda b, h, i: (b, Z_KB // dh + h)),
                  pl.BlockSpec((seq, dh), lambda b, h, i: (b, Z_VB // dh + h)),
                  pl.BlockSpec((1, dh), lambda b, h, i: (0, 0)),
                  pl.BlockSpec((1, dh), lambda b, h, i: (0, 0))],
        out_specs=pl.BlockSpec((BK, dh), lambda b, h, i: (b * nq + i, h)),
        out_shape=jax.ShapeDtypeStruct((t, MOBA_WIDTH), BF16),
        scratch_shapes=[pltpu.VMEM((seq, dh), BF16),
                        pltpu.VMEM((LANES, dh), F32)],
        compiler_params=_params("arbitrary", "arbitrary", "arbitrary"),
        name="moba",
    )(z, z, z, g_q, g_k)


def _mix_kernel(hm_ref, ob_ref, ga_ref, gb_ref, wa_ref, wb_ref, wo_ref, x_ref, mod_ref,
                out_ref, y_scr):
    @pl.when(pl.program_id(1) == 0)
    def _():
        ya = jnp.dot(hm_ref[...], wa_ref[...], preferred_element_type=F32)
        yb = jnp.dot(ob_ref[...], wb_ref[...], preferred_element_type=F32)
        y = (_sigmoid(ga_ref[...].astype(F32)) * ya + _sigmoid(gb_ref[...].astype(F32)) * yb)
        y_scr[...] = y.astype(BF16)

    proj = jnp.dot(y_scr[...], wo_ref[...], preferred_element_type=F32)
    out_ref[...] = x_ref[...] + mod_ref[2:3, :] * proj


def _mix_out(hm, ob, z, wa, wb, wo, x2, mod3, seq):
    t, d = x2.shape
    tm, tn = 512, 512
    per_b = seq // tm
    return pl.pallas_call(
        _mix_kernel,
        grid=(t // tm, d // tn),
        in_specs=[pl.BlockSpec((tm, MLSTM_WIDTH), lambda i, j: (i, 0)),
                  pl.BlockSpec((tm, MOBA_WIDTH), lambda i, j: (i, 0)),
                  pl.BlockSpec((tm, d), lambda i, j: (i, Z_GA // d)),
                  pl.BlockSpec((tm, d), lambda i, j: (i, Z_GB // d)),
                  pl.BlockSpec((MLSTM_WIDTH, d), lambda i, j: (0, 0)),
                  pl.BlockSpec((MOBA_WIDTH, d), lambda i, j: (0, 0)),
                  pl.BlockSpec((d, tn), lambda i, j: (0, j)),
                  pl.BlockSpec((tm, tn), lambda i, j: (i, j)),
                  pl.BlockSpec((None, 8, tn), lambda i, j: (i // per_b, 0, j))],
        out_specs=pl.BlockSpec((tm, tn), lambda i, j: (i, j)),
        out_shape=jax.ShapeDtypeStruct((t, d), F32),
        scratch_shapes=[pltpu.VMEM((tm, d), BF16)],
        compiler_params=_params("arbitrary", "arbitrary"),
        name="mix_out",
    )(hm, ob, z, z, wa, wb, wo, x2, mod3)


def _route_kernel(x_ref, mod_ref, g_ref, wr_ref, br_ref, t_ref, ri_ref, rw_ref, cnt_ref, cnt_scr):
    i = pl.program_id(0)

    @pl.when(i == 0)
    def _():
        cnt_scr[...] = jnp.zeros_like(cnt_scr)

    y = _rms(x_ref[...]) * g_ref[...]
    tt = y * (1.0 + mod_ref[4:5, :]) + mod_ref[3:4, :]
    tm = tt.shape[0]
    for s in range(ROW_CHUNKS):
        t_ref[pl.ds(s, tm, stride=ROW_CHUNKS), :] = tt[:, s * LANES:(s + 1) * LANES]

    logits = jnp.dot(tt, wr_ref[...], preferred_element_type=F32,
                     precision=lax.Precision.HIGHEST) + br_ref[...]
    lane = lax.broadcasted_iota(I32, (tm, LANES), 1)
    work = logits
    picks = []
    for _ in range(TOP_K):
        mx = jnp.max(work, axis=-1, keepdims=True)
        first = jnp.min(jnp.where(work == mx, lane, LANES), axis=-1, keepdims=True)
        picks.append((first, mx))
        work = jnp.where(lane == first, -jnp.inf, work)
    top = picks[0][1]
    exps = [jnp.exp(v - top) for _, v in picks]
    denom = exps[0] + exps[1] + exps[2] + exps[3]

    sel = jnp.zeros((tm, LANES), F32)
    for first, _ in picks:
        sel = sel + jnp.where(lane == first, 1.0, 0.0)
    row = lax.broadcasted_iota(I32, (tm, tm), 0)
    col = lax.broadcasted_iota(I32, (tm, tm), 1)
    strict = jnp.where(col < row, 1.0, 0.0).astype(BF16)
    before = jnp.dot(strict, sel.astype(BF16), preferred_element_type=F32) + cnt_scr[...]

    ri = jnp.zeros((tm, LANES), I32)
    rw = jnp.zeros((tm, LANES), F32)
    for kk, (first, _) in enumerate(picks):
        rank = jnp.sum(jnp.where(lane == first, before, 0.0), axis=-1, keepdims=True).astype(I32)
        ri = jnp.where(lane == kk, first, ri)
        ri = jnp.where(lane == TOP_K + kk, rank, ri)
        rw = jnp.where(lane == kk, exps[kk] / denom, rw)
    ri_ref[...] = ri
    rw_ref[...] = rw
    cnt_scr[...] = cnt_scr[...] + jnp.sum(sel, axis=0, keepdims=True)
    cnt_ref[...] = jnp.broadcast_to(cnt_scr[...], cnt_ref.shape)


def _route(x1, mod3, g_ffn, w_router_pad, b_router_pad, seq):
    t, d = x1.shape
    tm = 512
    per_b = seq // tm
    return pl.pallas_call(
        _route_kernel,
        grid=(t // tm,),
        in_specs=[pl.BlockSpec((tm, d), lambda i: (i, 0)),
                  pl.BlockSpec((None, 8, d), lambda i: (i // per_b, 0, 0)),
                  pl.BlockSpec((1, d), lambda i: (0, 0)),
                  pl.BlockSpec((d, LANES), lambda i: (0, 0)),
                  pl.BlockSpec((1, LANES), lambda i: (0, 0))],
        out_specs=[pl.BlockSpec((tm * ROW_CHUNKS, LANES), lambda i: (i, 0)),
                   pl.BlockSpec((tm, LANES), lambda i: (i, 0)),
                   pl.BlockSpec((tm, LANES), lambda i: (i, 0)),
                   pl.BlockSpec((8, LANES), lambda i: (0, 0))],
        out_shape=[jax.ShapeDtypeStruct((t * ROW_CHUNKS, LANES), F32),
                   jax.ShapeDtypeStruct((t, LANES), I32),
                   jax.ShapeDtypeStruct((t, LANES), F32),
                   jax.ShapeDtypeStruct((8, LANES), F32)],
        scratch_shapes=[pltpu.VMEM((1, LANES), F32)],
        compiler_params=_params("arbitrary"),
        name="route",
    )(x1, mod3, g_ffn, w_router_pad, b_router_pad)


def _inv_kernel(pos_ref, src_ref):
    n_rows = src_ref.shape[0]
    n_pairs = pos_ref.shape[0]

    def zero(i, c):
        src_ref[i] = 0
        return c

    lax.fori_loop(0, n_rows, zero, 0, unroll=8)

    def put(i, c):
        src_ref[pos_ref[i]] = lax.shift_right_logical(i, 2)
        return c

    lax.fori_loop(0, n_pairs, put, 0, unroll=8)


def _inverse(pos_flat, n_rows):
    return pl.pallas_call(
        _inv_kernel,
        in_specs=[pl.BlockSpec(memory_space=pltpu.SMEM)],
        out_specs=pl.BlockSpec(memory_space=pltpu.SMEM),
        out_shape=jax.ShapeDtypeStruct((n_rows,), I32),
        name="inverse",
    )(pos_flat)


def _row_copy(src_hbm, tok, buf, slot, r, sem):
    return pltpu.make_async_copy(src_hbm.at[pl.ds(tok * ROW_CHUNKS, ROW_CHUNKS), :],
                                 buf.at[slot, pl.ds(r * ROW_CHUNKS, ROW_CHUNKS), :],
                                 sem.at[slot])


def _dispatch_kernel(src_ref, nt_ref, t_hbm, out_ref, buf, sem):
    i = pl.program_id(0)
    nt = nt_ref[0]
    tm = out_ref.shape[0]

    def issue(tile, slot):
        def body(h, c):
            for par in range(2):
                r = 2 * h + par
                tok = src_ref[tile * tm + r]
                _row_copy(t_hbm, tok, buf, slot, r, sem).start(priority=par)
            return c
        lax.fori_loop(0, tm // 2, body, 0, unroll=4)

    @pl.when(i == 0)
    def _():
        issue(0, 0)

    @pl.when(i + 1 < nt)
    def _():
        issue(i + 1, (i + 1) % 2)

    @pl.when(i < nt)
    def _():
        slot = i % 2
        pltpu.make_async_copy(t_hbm.at[pl.ds(0, tm * ROW_CHUNKS), :], buf.at[slot], sem.at[slot]).wait()
        for s in range(ROW_CHUNKS):
            chunk = buf[slot, pl.ds(s, tm, stride=ROW_CHUNKS), :]
            out_ref[:, s * LANES:(s + 1) * LANES] = chunk.astype(out_ref.dtype)

    @pl.when(i >= nt)
    def _():
        out_ref[...] = jnp.zeros_like(out_ref)


def _dispatch(src, nt, t_rows, n_rows):
    tm = MOE_TM
    grid_spec = pltpu.PrefetchScalarGridSpec(
        num_scalar_prefetch=2,
        grid=(n_rows // tm,),
        in_specs=[pl.BlockSpec(memory_space=pl.ANY)],
        out_specs=pl.BlockSpec((tm, D_MODEL), lambda i, src, nt: (i, 0)),
        scratch_shapes=[pltpu.VMEM((2, tm * ROW_CHUNKS, LANES), F32),
                        pltpu.SemaphoreType.DMA((2,))],
    )
    return pl.pallas_call(
        _dispatch_kernel,
        grid_spec=grid_spec,
        out_shape=jax.ShapeDtypeStruct((n_rows, D_MODEL), BF16),
        compiler_params=_params("arbitrary"),
        name="dispatch",
    )(src, nt, t_rows)


def _gmm1_kernel(te_ref, nx_ref, nt_ref, x_ref, w_hbm, bg_ref, bl_ref, act_ref,
                 wf_buf, wg_scr, wl_scr, sem, *, nj):
    j = pl.program_id(0)
    i = pl.program_id(1)
    tj = act_ref.shape[1]

    def copies(e, jj):
        return [pltpu.make_async_copy(w_hbm.at[e, :, (h * nj + jj) * tj:(h * nj + jj + 1) * tj],
                                      wf_buf.at[h], sem.at[h]) for h in range(2)]

    def start(e, jdyn):
        for jj in range(nj):
            @pl.when(jdyn == jj)
            def _(jj=jj):
                for cp in copies(e, jj):
                    cp.start()

    @pl.when(i < nt_ref[0])
    def _():
        e = te_ref[i]
        new_group = (i == 0) | (e != te_ref[jnp.maximum(i - 1, 0)])

        @pl.when((i == 0) & (j == 0))
        def _():
            start(e, j)

        @pl.when(new_group)
        def _():
            for cp in copies(e, 0):
                cp.wait()
            wg_scr[...] = wf_buf[0].astype(BF16)
            wl_scr[...] = wf_buf[1].astype(BF16)
            nxt = nx_ref[i]

            @pl.when(nxt >= 0)
            def _():
                start(nxt, j)

            @pl.when((nxt < 0) & (j + 1 < nj))
            def _():
                start(te_ref[0], j + 1)

        x = x_ref[...]
        glu = jnp.dot(x, wg_scr[...], preferred_element_type=F32) + bg_ref[...]
        lin = jnp.dot(x, wl_scr[...], preferred_element_type=F32) + bl_ref[...]
        glu = jnp.minimum(glu, SWIGLU_LIMIT)
        lin = jnp.clip(lin, -SWIGLU_LIMIT, SWIGLU_LIMIT)
        act_ref[...] = (glu * _sigmoid(SWIGLU_ALPHA * glu) * (lin + 1.0)).astype(act_ref.dtype)

    @pl.when(i >= nt_ref[0])
    def _():
        act_ref[...] = jnp.zeros_like(act_ref)


def _gmm1(te, nx, nt, xs, w_up, b_up3):
    n_rows, d = xs.shape
    de = w_up.shape[2] // 2
    tm, tj = MOE_TM, 1024
    nj = de // tj
    row = lambda j, i, te, nx, nt: jnp.maximum(jnp.minimum(i, nt[0] - 1), 0)
    exp = lambda j, i, te, nx, nt: te[row(j, i, te, nx, nt)]
    grid_spec = pltpu.PrefetchScalarGridSpec(
        num_scalar_prefetch=3,
        grid=(nj, n_rows // tm),
        in_specs=[pl.BlockSpec((tm, d), lambda j, i, te, nx, nt: (row(j, i, te, nx, nt), 0)),
                  pl.BlockSpec(memory_space=pl.ANY),
                  pl.BlockSpec((None, 1, tj), lambda j, i, te, nx, nt: (exp(j, i, te, nx, nt), 0, j)),
                  pl.BlockSpec((None, 1, tj), lambda j, i, te, nx, nt: (exp(j, i, te, nx, nt), 0, nj + j))],
        out_specs=pl.BlockSpec((tm, tj), lambda j, i, te, nx, nt: (i, j)),
        scratch_shapes=[pltpu.VMEM((2, d, tj), F32), pltpu.VMEM((d, tj), BF16), pltpu.VMEM((d, tj), BF16),
                        pltpu.SemaphoreType.DMA((2,))],
    )
    return pl.pallas_call(
        functools.partial(_gmm1_kernel, nj=nj),
        grid_spec=grid_spec,
        out_shape=jax.ShapeDtypeStruct((n_rows, de), BF16),
        compiler_params=_params("arbitrary", "arbitrary"),
        name="gmm1",
    )(te, nx, nt, xs, w_up, b_up3, b_up3)


def _gmm2_kernel(te_ref, nx_ref, nt_ref, a_ref, wd_hbm, bd_ref, y_ref, wf_buf, wd_scr, sem):
    i = pl.program_id(0)

    def fetch(e):
        return pltpu.make_async_copy(wd_hbm.at[e], wf_buf, sem.at[0])

    @pl.when(i < nt_ref[0])
    def _():
        e = te_ref[i]
        new_group = (i == 0) | (e != te_ref[jnp.maximum(i - 1, 0)])

        @pl.when(i == 0)
        def _():
            fetch(e).start()

        @pl.when(new_group)
        def _():
            fetch(e).wait()
            wd_scr[...] = wf_buf[...].astype(BF16)
            nxt = nx_ref[i]

            @pl.when(nxt >= 0)
            def _():
                fetch(nxt).start()

        y = jnp.dot(a_ref[...], wd_scr[...], preferred_element_type=F32) + bd_ref[...]
        tm = y.shape[0]
        for s in range(ROW_CHUNKS):
            y_ref[pl.ds(s, tm, stride=ROW_CHUNKS), :] = y[:, s * LANES:(s + 1) * LANES]

    @pl.when(i >= nt_ref[0])
    def _():
        y_ref[...] = jnp.zeros_like(y_ref)


def _gmm2(te, nx, nt, act, w_down, b_down3):
    n_rows, de = act.shape
    d = w_down.shape[2]
    tm = MOE_TM
    row = lambda i, te, nx, nt: jnp.maximum(jnp.minimum(i, nt[0] - 1), 0)
    exp = lambda i, te, nx, nt: te[row(i, te, nx, nt)]
    grid_spec = pltpu.PrefetchScalarGridSpec(
        num_scalar_prefetch=3,
        grid=(n_rows // tm,),
        in_specs=[pl.BlockSpec((tm, de), lambda i, te, nx, nt: (row(i, te, nx, nt), 0)),
                  pl.BlockSpec(memory_space=pl.ANY),
                  pl.BlockSpec((None, 1, d), lambda i, te, nx, nt: (exp(i, te, nx, nt), 0, 0))],
        out_specs=pl.BlockSpec((tm * ROW_CHUNKS, LANES), lambda i, te, nx, nt: (i, 0)),
        scratch_shapes=[pltpu.VMEM((de, d), F32), pltpu.VMEM((de, d), BF16),
                        pltpu.SemaphoreType.DMA((1,))],
    )
    return pl.pallas_call(
        _gmm2_kernel,
        grid_spec=grid_spec,
        out_shape=jax.ShapeDtypeStruct((n_rows * ROW_CHUNKS, LANES), F32),
        compiler_params=_params("arbitrary"),
        name="gmm2",
    )(te, nx, nt, act, w_down, b_down3)


def _combine_kernel(pos_ref, y_hbm, rw_ref, x_ref, mod_ref, out_ref, buf, sem):
    i = pl.program_id(0)
    n = pl.num_programs(0)
    tm = out_ref.shape[0]

    def issue(tile, slot):
        def body(r, c):
            for kk in range(TOP_K):
                p = pos_ref[(tile * tm + r) * TOP_K + kk]
                pltpu.make_async_copy(y_hbm.at[pl.ds(p * ROW_CHUNKS, ROW_CHUNKS), :],
                                      buf.at[slot, kk, pl.ds(r * ROW_CHUNKS, ROW_CHUNKS), :],
                                      sem.at[slot]).start(priority=kk % 2)
            return c
        lax.fori_loop(0, tm, body, 0, unroll=4)

    @pl.when(i == 0)
    def _():
        issue(0, 0)

    @pl.when(i + 1 < n)
    def _():
        issue(i + 1, (i + 1) % 2)

    slot = i % 2
    for kk in range(TOP_K):
        pltpu.make_async_copy(y_hbm.at[pl.ds(0, tm * ROW_CHUNKS), :], buf.at[slot, kk],
                              sem.at[slot]).wait()
    rw = rw_ref[...]
    gate = mod_ref[5:6, :]
    for s in range(ROW_CHUNKS):
        acc = None
        for kk in range(TOP_K):
            chunk = buf[slot, kk, pl.ds(s, tm, stride=ROW_CHUNKS), :]
            term = rw[:, kk:kk + 1] * chunk
            acc = term if acc is None else acc + term
        sl = slice(s * LANES, (s + 1) * LANES)
        out_ref[:, sl] = x_ref[:, sl] + gate[:, sl] * acc


def _combine(pos_flat, y_rows, rw, x1, mod3, seq):
    t, d = x1.shape
    tm = 256
    per_b = seq // tm
    grid_spec = pltpu.PrefetchScalarGridSpec(
        num_scalar_prefetch=1,
        grid=(t // tm,),
        in_specs=[pl.BlockSpec(memory_space=pl.ANY),
                  pl.BlockSpec((tm, LANES), lambda i, pos: (i, 0)),
                  pl.BlockSpec((tm, d), lambda i, pos: (i, 0)),
                  pl.BlockSpec((None, 8, d), lambda i, pos: (i // per_b, 0, 0))],
        out_specs=pl.BlockSpec((tm, d), lambda i, pos: (i, 0)),
        scratch_shapes=[pltpu.VMEM((2, TOP_K, tm * ROW_CHUNKS, LANES), F32),
                        pltpu.SemaphoreType.DMA((2,))],
    )
    return pl.pallas_call(
        _combine_kernel,
        grid_spec=grid_spec,
        out_shape=jax.ShapeDtypeStruct((t, d), F32),
        compiler_params=_params("arbitrary"),
        name="combine",
    )(pos_flat, y_rows, rw, x1, mod3)


def _moe(x1, mod3, g_ffn, w_router, b_router, w_up, b_up, w_down, b_down, seq):
    t, d = x1.shape
    ne = w_up.shape[0]
    tm = MOE_TM
    n_rows = t * TOP_K + ne * tm
    n_tiles = n_rows // tm

    wr = jnp.pad(w_router, ((0, 0), (0, LANES - ne)))
    br = jnp.concatenate([b_router, jnp.full((LANES - ne,), NEG_INF, F32)]).reshape(1, LANES)
    t_rows, ri, rw, cnt = _route(x1, mod3, g_ffn.reshape(1, d), wr, br, seq)

    counts = cnt[0, :ne].astype(I32)
    tiles_per = (counts + tm - 1) // tm
    tile_end = jnp.cumsum(tiles_per)
    offs = (tile_end - tiles_per) * tm
    eidx = ri[:, :TOP_K]
    rank = ri[:, TOP_K:2 * TOP_K]
    onehot = eidx[..., None] == jnp.arange(ne, dtype=I32)
    pos = (jnp.sum(jnp.where(onehot, offs, 0), axis=-1) + rank).reshape(-1)
    nt = tile_end[-1:].astype(I32)
    te = jnp.minimum(jnp.sum(jnp.arange(n_tiles, dtype=I32)[:, None] >= tile_end[None, :], axis=1),
                     ne - 1).astype(I32)

    next_tile = jnp.sum(jnp.where(te[:, None] == jnp.arange(ne, dtype=I32), tile_end[None, :], 0), axis=1)
    nx = jnp.where(next_tile < nt[0], te[jnp.minimum(next_tile, n_tiles - 1)], -1).astype(I32)

    src = _inverse(pos, n_rows)
    xs = _dispatch(src, nt, t_rows, n_rows)
    act = _gmm1(te, nx, nt, xs, w_up, b_up.reshape(ne, 1, -1))
    y_rows = _gmm2(te, nx, nt, act, w_down, b_down.reshape(ne, 1, d))
    return _combine(pos, y_rows, rw, x1, mod3, seq)


def _layer(x2, c, batch, seq, w_ada, b_ada, g_mix, w_in, b_igate, b_fgate, g_mlstm_out, g_q, g_k,
           w_branch_a, w_branch_b, w_out, g_ffn, w_router, b_router, w_up, b_up, w_down, b_down):
    d = D_MODEL
    c_pad = jnp.zeros((8, d), F32).at[:batch].set(c)
    mod = _ada(c_pad, w_ada, b_ada)
    mod3 = jnp.pad(mod[:batch].reshape(batch, 6, d), ((0, 0), (0, 2), (0, 0)))

    i0 = Z_QB
    w_main = jnp.concatenate([w_in[:, :i0], w_in[:, i0 + 2 * MLSTM_HEADS:]], axis=1).astype(BF16)
    w_gate = jnp.pad(w_in[:, i0:i0 + 2 * MLSTM_HEADS],
                     ((0, 0), (0, LANES - 2 * MLSTM_HEADS))).astype(BF16)
    gbias = jnp.pad(jnp.concatenate([b_igate, b_fgate]), (0, LANES - 2 * MLSTM_HEADS)).reshape(1, LANES)

    z, gates = _in_proj(x2, mod3, g_mix.reshape(1, d), w_main, w_gate, seq)
    hm = _mlstm(z, gates, gbias, g_mlstm_out, batch, seq)
    ob = _moba(z, g_q.reshape(1, -1), g_k.reshape(1, -1), batch, seq)
    x1 = _mix_out(hm, ob, z, w_branch_a.astype(BF16), w_branch_b.astype(BF16), w_out.astype(BF16),
                  x2, mod3, seq)
    return _moe(x1, mod3, g_ffn, w_router, b_router, w_up, b_up, w_down, b_down, seq)


def kernel(x, c, w_ada, b_ada, g_mix, w_in, b_igate, b_fgate, g_mlstm_out, g_q, g_k, w_branch_a,
           w_branch_b, w_out, g_ffn, w_router, b_router, w_up, b_up, w_down, b_down):
    batch, seq, d = x.shape
    x2 = x.reshape(batch * seq, d)
    for l in range(w_ada.shape[0]):
        x2 = _layer(x2, c, batch, seq, w_ada[l], b_ada[l], g_mix[l], w_in[l], b_igate[l], b_fgate[l],
                    g_mlstm_out[l], g_q[l], g_k[l], w_branch_a[l], w_branch_b[l], w_out[l], g_ffn[l],
                    w_router[l], b_router[l], w_up[l], b_up[l], w_down[l], b_down[l])
    return x2.reshape(batch, seq, d)
```

```python
import functools

import jax
import jax.numpy as jnp
from jax import lax
from jax.experimental import pallas as pl
from jax.experimental.pallas import tpu as pltpu

F32 = jnp.float32
BF16 = jnp.bfloat16
I32 = jnp.int32

D_MODEL = 2048
MLSTM_HEADS = 8
MLSTM_DV = 128
MLSTM_DK = 64
MLSTM_WIDTH = MLSTM_HEADS * MLSTM_DV
GATE_SOFTCAP = 15.0
MOBA_HEADS = 8
MOBA_HEAD_DIM = 128
MOBA_BLOCK = 256
MOBA_TOPK = 3
MOBA_WIDTH = MOBA_HEADS * MOBA_HEAD_DIM
N_EXPERTS = 32
TOP_K = 4
D_EXPERT = D_MODEL
SWIGLU_ALPHA = 1.702
SWIGLU_LIMIT = 7.0
NORM_EPS = 1e-6
NEG_INF = -1e30

LANES = 128
VMEM_LIMIT = 56 * 1024 * 1024

Z_QA, Z_KA, Z_VA, Z_OA = 0, 512, 1024, 2048
Z_QB, Z_KB, Z_VB = 3072, 4096, 5120
Z_GA, Z_GB = 6144, 8192
Z_WIDTH = 10240

MLSTM_CHUNK = 256

MOE_TM = 256
ROW_CHUNKS = D_MODEL // LANES


def _params(*sem):
    return pltpu.CompilerParams(dimension_semantics=sem, vmem_limit_bytes=VMEM_LIMIT)


def _rms(x, eps=NORM_EPS):
    return x * lax.rsqrt(jnp.mean(x * x, axis=-1, keepdims=True) + eps)


def _sigmoid(x):
    return 1.0 / (1.0 + jnp.exp(-x))


def _ada_kernel(c_ref, w_ref, b_ref, o_ref):
    c = c_ref[...]
    s = (c * _sigmoid(c)).astype(BF16)
    o_ref[...] = jnp.dot(s, w_ref[...].astype(BF16), preferred_element_type=F32) + b_ref[...]


def _ada(c_pad, w_ada, b_ada):
    rows, d = c_pad.shape
    n = w_ada.shape[1]
    tn = 1024
    return pl.pallas_call(
        _ada_kernel,
        grid=(n // tn,),
        in_specs=[pl.BlockSpec((rows, d), lambda j: (0, 0)),
                  pl.BlockSpec((d, tn), lambda j: (0, j)),
                  pl.BlockSpec((1, tn), lambda j: (0, j))],
        out_specs=pl.BlockSpec((rows, tn), lambda j: (0, j)),
        out_shape=jax.ShapeDtypeStruct((rows, n), F32),
        compiler_params=_params("arbitrary"),
        name="ada",
    )(c_pad, w_ada, b_ada.reshape(1, n))


def _in_kernel(x_ref, mod_ref, g_ref, w_ref, wg_ref, z_ref, gates_ref, h_scr):
    @pl.when(pl.program_id(1) == 0)
    def _():
        y = _rms(x_ref[...]) * g_ref[...]
        h = y * (1.0 + mod_ref[1:2, :]) + mod_ref[0:1, :]
        hb = h.astype(BF16)
        h_scr[...] = hb
        gates_ref[...] = jnp.dot(hb, wg_ref[...], preferred_element_type=F32)

    z_ref[...] = jnp.dot(h_scr[...], w_ref[...], preferred_element_type=F32).astype(z_ref.dtype)


def _in_proj(x2, mod3, g_mix, w_main, w_gate, seq):
    t, d = x2.shape
    n = w_main.shape[1]
    tm, tn = 1024, 2048
    per_b = seq // tm
    return pl.pallas_call(
        _in_kernel,
        grid=(t // tm, n // tn),
        in_specs=[pl.BlockSpec((tm, d), lambda i, j: (i, 0)),
                  pl.BlockSpec((None, 8, d), lambda i, j: (i // per_b, 0, 0)),
                  pl.BlockSpec((1, d), lambda i, j: (0, 0)),
                  pl.BlockSpec((d, tn), lambda i, j: (0, j)),
                  pl.BlockSpec((d, LANES), lambda i, j: (0, 0))],
        out_specs=[pl.BlockSpec((tm, tn), lambda i, j: (i, j)),
                   pl.BlockSpec((tm, LANES), lambda i, j: (i, 0))],
        out_shape=[jax.ShapeDtypeStruct((t, n), BF16),
                   jax.ShapeDtypeStruct((t, LANES), F32)],
        scratch_shapes=[pltpu.VMEM((tm, d), BF16)],
        compiler_params=_params("arbitrary", "arbitrary"),
        name="in_proj",
    )(x2, mod3, g_mix, w_main, w_gate)


def _mlstm_kernel(q_ref, k_ref, v_ref, o_ref, gates_ref, gbias_ref, gout_ref, out_ref,
                  c_scr, n_scr, m_scr):
    L = MLSTM_CHUNK
    pair = pl.program_id(1)

    @pl.when(pl.program_id(2) == 0)
    def _():
        c_scr[...] = jnp.zeros_like(c_scr)
        n_scr[...] = jnp.zeros_like(n_scr)
        m_scr[...] = jnp.zeros_like(m_scr)

    pre = gates_ref[...] + gbias_ref[...]
    capped = GATE_SOFTCAP * jnp.tanh(pre / GATE_SOFTCAP)
    log_f = jnp.minimum(capped, 0.0) - jnp.log1p(jnp.exp(-jnp.abs(capped)))
    row = lax.broadcasted_iota(jnp.int32, (L, L), 0)
    col = lax.broadcasted_iota(jnp.int32, (L, L), 1)
    causal = col <= row
    tril = jnp.where(causal, 1.0, 0.0).astype(F32)
    b_all = jnp.dot(tril, log_f, preferred_element_type=F32,
                    precision=lax.Precision.HIGHEST)
    li_t = capped.T
    b_t = b_all.T

    for j in range(2):
        head = 2 * pair + j
        q = q_ref[:, j * MLSTM_DK:(j + 1) * MLSTM_DK] * (MLSTM_DK ** -0.5)
        k = k_ref[:, j * MLSTM_DK:(j + 1) * MLSTM_DK]
        v = v_ref[:, j * MLSTM_DV:(j + 1) * MLSTM_DV]
        lane = lax.broadcasted_iota(jnp.int32, (L, LANES), 1)
        sub = lax.broadcasted_iota(jnp.int32, (LANES, L), 0)
        li_c = jnp.sum(jnp.where(lane == head, capped, 0.0), axis=1, keepdims=True)
        b_c = jnp.sum(jnp.where(lane == head + MLSTM_HEADS, b_all, 0.0), axis=1, keepdims=True)
        li_r = jnp.sum(jnp.where(sub == head, li_t, 0.0), axis=0, keepdims=True)
        b_r = jnp.sum(jnp.where(sub == head + MLSTM_HEADS, b_t, 0.0), axis=0, keepdims=True)
        ct = c_scr[j]
        nvec = n_scr[j]
        m_in = m_scr[j][:, :1]
        g = b_c[L - 1:L, :]

        d = jnp.where(causal, b_c - b_r + li_r, NEG_INF)
        dmax = jnp.max(d, axis=-1, keepdims=True)
        inter = b_c + m_in
        m_j = jnp.maximum(inter, dmax)
        p = jnp.exp(d - m_j)
        s = lax.dot_general(q, k, (((1,), (1,)), ((), ())), preferred_element_type=F32) * p
        s_inter = jnp.exp(inter - m_j)
        qf = q.astype(F32)
        num = (jnp.dot(s.astype(BF16), v, preferred_element_type=F32)
               + s_inter * jnp.dot(q, ct.astype(BF16), preferred_element_type=F32))
        den = (jnp.sum(s, axis=-1, keepdims=True)
               + s_inter * jnp.sum(qf * nvec, axis=-1, keepdims=True))
        h = num / jnp.maximum(jnp.abs(den), jnp.exp(-m_j))

        gout = gout_ref[pl.ds(head, 1), :]
        hn = _rms(h) * gout
        og = o_ref[:, j * MLSTM_DV:(j + 1) * MLSTM_DV].astype(F32)
        out_ref[:, j * MLSTM_DV:(j + 1) * MLSTM_DV] = (hn * _sigmoid(og)).astype(out_ref.dtype)

        a = g - b_c + li_c
        m_loc = jnp.max(a, axis=0, keepdims=True)
        w = jnp.exp(a - m_loc)
        kw = k.astype(F32) * w
        c_loc = lax.dot_general(kw.astype(BF16), v, (((0,), (0,)), ((), ())),
                                preferred_element_type=F32)
        n_loc = jnp.sum(kw, axis=0, keepdims=True)
        m_new = jnp.maximum(g + m_in, m_loc)
        s_prev = jnp.exp(g + m_in - m_new)
        s_loc = jnp.exp(m_loc - m_new)
        c_scr[j] = s_prev * ct + s_loc * c_loc
        n_scr[j] = s_prev * nvec + s_loc * n_loc
        m_scr[j] = jnp.broadcast_to(m_new, (1, LANES))


def _mlstm(z, gates, gbias, g_out, batch, seq):
    t = z.shape[0]
    L = MLSTM_CHUNK
    nc = seq // L
    pairs = MLSTM_HEADS // 2
    rowmap = lambda b, p, c: b * nc + c
    return pl.pallas_call(
        _mlstm_kernel,
        grid=(batch, pairs, nc),
        in_specs=[pl.BlockSpec((L, 128), lambda b, p, c: (rowmap(b, p, c), Z_QA // 128 + p)),
                  pl.BlockSpec((L, 128), lambda b, p, c: (rowmap(b, p, c), Z_KA // 128 + p)),
                  pl.BlockSpec((L, 256), lambda b, p, c: (rowmap(b, p, c), Z_VA // 256 + p)),
                  pl.BlockSpec((L, 256), lambda b, p, c: (rowmap(b, p, c), Z_OA // 256 + p)),
                  pl.BlockSpec((L, LANES), lambda b, p, c: (rowmap(b, p, c), 0)),
                  pl.BlockSpec((1, LANES), lambda b, p, c: (0, 0)),
                  pl.BlockSpec((MLSTM_HEADS, MLSTM_DV), lambda b, p, c: (0, 0))],
        out_specs=pl.BlockSpec((L, 256), lambda b, p, c: (rowmap(b, p, c), p)),
        out_shape=jax.ShapeDtypeStruct((t, MLSTM_WIDTH), BF16),
        scratch_shapes=[pltpu.VMEM((2, MLSTM_DK, MLSTM_DV), F32),
                        pltpu.VMEM((2, 1, MLSTM_DK), F32),
                        pltpu.VMEM((2, 1, LANES), F32)],
        compiler_params=_params("arbitrary", "arbitrary", "arbitrary"),
        name="mlstm",
    )(z, z, z, z, gates, gbias, g_out)


def _moba_kernel(q_ref, k_ref, v_ref, gq_ref, gk_ref, out_ref, kn_scr, kmean_scr):
    BK = MOBA_BLOCK
    nb = k_ref.shape[0] // BK
    qi = pl.program_id(2)
    scale = MOBA_HEAD_DIM ** -0.5

    @pl.when(qi == 0)
    def _():
        kmean_scr[...] = jnp.zeros_like(kmean_scr)
        for n in range(nb):
            kf = k_ref[n * BK:(n + 1) * BK, :].astype(F32)
            kn = _rms(kf) * gk_ref[...]
            kn_scr[n * BK:(n + 1) * BK, :] = kn.astype(BF16)
            kmean_scr[n:n + 1, :] = jnp.mean(kn, axis=0, keepdims=True)

    qn = _rms(q_ref[...].astype(F32)) * gq_ref[...]
    qb = qn.astype(BF16)
    gate = lax.dot_general(qn, kmean_scr[...], (((1,), (1,)), ((), ())),
                           preferred_element_type=F32,
                           precision=lax.Precision.HIGHEST)
    lane = lax.broadcasted_iota(jnp.int32, (BK, LANES), 1)
    cnt = jnp.zeros((BK, LANES), F32)
    for m in range(nb):
        gm = gate[:, m:m + 1]
        beats = (gm > gate) | ((gm == gate) & (m < lane))
        cnt = cnt + jnp.where(beats & (m < qi), 1.0, 0.0)
    sel = jnp.where((cnt < MOBA_TOPK) & (lane < qi), 1.0, 0.0)

    row = lax.broadcasted_iota(jnp.int32, (BK, BK), 0)
    col = lax.broadcasted_iota(jnp.int32, (BK, BK), 1)
    causal = jnp.where(col <= row, 1.0, 0.0)

    def attend(n_blocks):
        nk = n_blocks * BK
        s = lax.dot_general(qb, kn_scr[0:nk, :], (((1,), (1,)), ((), ())),
                            preferred_element_type=F32) * scale
        allowed = jnp.concatenate(
            [jnp.where(qi == n, causal, jnp.broadcast_to(sel[:, n:n + 1], (BK, BK)))
             for n in range(n_blocks)], axis=1)
        s = jnp.where(allowed > 0.0, s, NEG_INF)
        p = jnp.exp(s - jnp.max(s, axis=-1, keepdims=True))
        num = jnp.dot(p.astype(BF16), v_ref[0:nk, :], preferred_element_type=F32)
        out_ref[...] = (num / jnp.sum(p, axis=-1, keepdims=True)).astype(out_ref.dtype)

    step = 2
    for hi in range(step, nb + 1, step):
        @pl.when((qi >= hi - step) & (qi < hi))
        def _(hi=hi):
            attend(hi)


def _moba(z, g_q, g_k, batch, seq):
    t = z.shape[0]
    BK = MOBA_BLOCK
    nq = seq // BK
    dh = MOBA_HEAD_DIM
    return pl.pallas_call(
        _moba_kernel,
        grid=(batch, MOBA_HEADS, nq),
        in_specs=[pl.BlockSpec((BK, dh), lambda b, h, i: (b * nq + i, Z_QB // dh + h)),
                  pl.BlockSpec((seq, dh), lambda b, h, i: (b, Z_KB // dh + h)),
                  pl.BlockSpec((seq, dh), lambda b, h, i: (b, Z_VB // dh + h)),
                  pl.BlockSpec((1, dh), lambda b, h, i: (0, 0)),
                  pl.BlockSpec((1, dh), lambda b, h, i: (0, 0))],
        out_specs=pl.BlockSpec((BK, dh), lambda b, h, i: (b * nq + i, h)),
        out_shape=jax.ShapeDtypeStruct((t, MOBA_WIDTH), BF16),
        scratch_shapes=[pltpu.VMEM((seq, dh), BF16),
                        pltpu.VMEM((LANES, dh), F32)],
        compiler_params=_params("arbitrary", "arbitrary", "arbitrary"),
        name="moba",
    )(z, z, z, g_q, g_k)


def _mix_kernel(hm_ref, ob_ref, ga_ref, gb_ref, wa_ref, wb_ref, wo_ref, x_ref, mod_ref,
                out_ref, y_scr):
    @pl.when(pl.program_id(1) == 0)
    def _():
        ya = jnp.dot(hm_ref[...], wa_ref[...], preferred_element_type=F32)
        yb = jnp.dot(ob_ref[...], wb_ref[...], preferred_element_type=F32)
        y = (_sigmoid(ga_ref[...].astype(F32)) * ya + _sigmoid(gb_ref[...].astype(F32)) * yb)
        y_scr[...] = y.astype(BF16)

    proj = jnp.dot(y_scr[...], wo_ref[...], preferred_element_type=F32)
    out_ref[...] = x_ref[...] + mod_ref[2:3, :] * proj


def _mix_out(hm, ob, z, wa, wb, wo, x2, mod3, seq):
    t, d = x2.shape
    tm, tn = 512, 1024
    per_b = seq // tm
    return pl.pallas_call(
        _mix_kernel,
        grid=(t // tm, d // tn),
        in_specs=[pl.BlockSpec((tm, MLSTM_WIDTH), lambda i, j: (i, 0)),
                  pl.BlockSpec((tm, MOBA_WIDTH), lambda i, j: (i, 0)),
                  pl.BlockSpec((tm, d), lambda i, j: (i, Z_GA // d)),
                  pl.BlockSpec((tm, d), lambda i, j: (i, Z_GB // d)),
                  pl.BlockSpec((MLSTM_WIDTH, d), lambda i, j: (0, 0)),
                  pl.BlockSpec((MOBA_WIDTH, d), lambda i, j: (0, 0)),
                  pl.BlockSpec((d, tn), lambda i, j: (0, j)),
                  pl.BlockSpec((tm, tn), lambda i, j: (i, j)),
                  pl.BlockSpec((None, 8, tn), lambda i, j: (i // per_b, 0, j))],
        out_specs=pl.BlockSpec((tm, tn), lambda i, j: (i, j)),
        out_shape=jax.ShapeDtypeStruct((t, d), F32),
        scratch_shapes=[pltpu.VMEM((tm, d), BF16)],
        compiler_params=_params("arbitrary", "arbitrary"),
        name="mix_out",
    )(hm, ob, z, z, wa, wb, wo, x2, mod3)


def _route_kernel(x_ref, mod_ref, g_ref, wr_ref, br_ref, t_ref, ri_ref, rw_ref, cnt_ref, cnt_scr):
    i = pl.program_id(0)

    @pl.when(i == 0)
    def _():
        cnt_scr[...] = jnp.zeros_like(cnt_scr)

    y = _rms(x_ref[...]) * g_ref[...]
    tt = y * (1.0 + mod_ref[4:5, :]) + mod_ref[3:4, :]
    tm = tt.shape[0]
    for s in range(ROW_CHUNKS):
        t_ref[pl.ds(s, tm, stride=ROW_CHUNKS), :] = tt[:, s * LANES:(s + 1) * LANES]

    logits = jnp.dot(tt, wr_ref[...], preferred_element_type=F32,
                     precision=lax.Precision.HIGHEST) + br_ref[...]
    lane = lax.broadcasted_iota(I32, (tm, LANES), 1)
    work = logits
    picks = []
    for _ in range(TOP_K):
        mx = jnp.max(work, axis=-1, keepdims=True)
        first = jnp.min(jnp.where(work == mx, lane, LANES), axis=-1, keepdims=True)
        picks.append((first, mx))
        work = jnp.where(lane == first, -jnp.inf, work)
    top = picks[0][1]
    exps = [jnp.exp(v - top) for _, v in picks]
    denom = exps[0] + exps[1] + exps[2] + exps[3]

    sel = jnp.zeros((tm, LANES), F32)
    for first, _ in picks:
        sel = sel + jnp.where(lane == first, 1.0, 0.0)
    row = lax.broadcasted_iota(I32, (tm, tm), 0)
    col = lax.broadcasted_iota(I32, (tm, tm), 1)
    strict = jnp.where(col < row, 1.0, 0.0).astype(BF16)
    before = jnp.dot(strict, sel.astype(BF16), preferred_element_type=F32) + cnt_scr[...]

    ri = jnp.zeros((tm, LANES), I32)
    rw = jnp.zeros((tm, LANES), F32)
    for kk, (first, _) in enumerate(picks):
        rank = jnp.sum(jnp.where(lane == first, before, 0.0), axis=-1, keepdims=True).astype(I32)
        ri = jnp.where(lane == kk, first, ri)
        ri = jnp.where(lane == TOP_K + kk, rank, ri)
        rw = jnp.where(lane == kk, exps[kk] / denom, rw)
    ri_ref[...] = ri
    rw_ref[...] = rw
    cnt_scr[...] = cnt_scr[...] + jnp.sum(sel, axis=0, keepdims=True)
    cnt_ref[...] = jnp.broadcast_to(cnt_scr[...], cnt_ref.shape)


def _route(x1, mod3, g_ffn, w_router_pad, b_router_pad, seq):
    t, d = x1.shape
    tm = 512
    per_b = seq // tm
    return pl.pallas_call(
        _route_kernel,
        grid=(t // tm,),
        in_specs=[pl.BlockSpec((tm, d), lambda i: (i, 0)),
                  pl.BlockSpec((None, 8, d), lambda i: (i // per_b, 0, 0)),
                  pl.BlockSpec((1, d), lambda i: (0, 0)),
                  pl.BlockSpec((d, LANES), lambda i: (0, 0)),
                  pl.BlockSpec((1, LANES), lambda i: (0, 0))],
        out_specs=[pl.BlockSpec((tm * ROW_CHUNKS, LANES), lambda i: (i, 0)),
                   pl.BlockSpec((tm, LANES), lambda i: (i, 0)),
                   pl.BlockSpec((tm, LANES), lambda i: (i, 0)),
                   pl.BlockSpec((8, LANES), lambda i: (0, 0))],
        out_shape=[jax.ShapeDtypeStruct((t * ROW_CHUNKS, LANES), F32),
                   jax.ShapeDtypeStruct((t, LANES), I32),
                   jax.ShapeDtypeStruct((t, LANES), F32),
                   jax.ShapeDtypeStruct((8, LANES), F32)],
        scratch_shapes=[pltpu.VMEM((1, LANES), F32)],
        compiler_params=_params("arbitrary"),
        name="route",
    )(x1, mod3, g_ffn, w_router_pad, b_router_pad)


def _inv_kernel(pos_ref, src_ref):
    n_rows = src_ref.shape[0]
    n_pairs = pos_ref.shape[0]

    def zero(i, c):
        src_ref[i] = 0
        return c

    lax.fori_loop(0, n_rows, zero, 0, unroll=8)

    def put(i, c):
        src_ref[pos_ref[i]] = lax.shift_right_logical(i, 2)
        return c

    lax.fori_loop(0, n_pairs, put, 0, unroll=8)


def _inverse(pos_flat, n_rows):
    return pl.pallas_call(
        _inv_kernel,
        in_specs=[pl.BlockSpec(memory_space=pltpu.SMEM)],
        out_specs=pl.BlockSpec(memory_space=pltpu.SMEM),
        out_shape=jax.ShapeDtypeStruct((n_rows,), I32),
        name="inverse",
    )(pos_flat)


def _row_copy(src_hbm, tok, buf, slot, r, sem):
    return pltpu.make_async_copy(src_hbm.at[pl.ds(tok * ROW_CHUNKS, ROW_CHUNKS), :],
                                 buf.at[slot, pl.ds(r * ROW_CHUNKS, ROW_CHUNKS), :],
                                 sem.at[slot])


def _dispatch_kernel(src_ref, nt_ref, t_hbm, out_ref, buf, sem):
    i = pl.program_id(0)
    nt = nt_ref[0]
    tm = out_ref.shape[0]

    def issue(tile, slot):
        def body(h, c):
            for par in range(2):
                r = 2 * h + par
                tok = src_ref[tile * tm + r]
                _row_copy(t_hbm, tok, buf, slot, r, sem).start(priority=par)
            return c
        lax.fori_loop(0, tm // 2, body, 0, unroll=4)

    @pl.when(i == 0)
    def _():
        issue(0, 0)

    @pl.when(i + 1 < nt)
    def _():
        issue(i + 1, (i + 1) % 2)

    @pl.when(i < nt)
    def _():
        slot = i % 2
        pltpu.make_async_copy(t_hbm.at[pl.ds(0, tm * ROW_CHUNKS), :], buf.at[slot], sem.at[slot]).wait()
        for s in range(ROW_CHUNKS):
            chunk = buf[slot, pl.ds(s, tm, stride=ROW_CHUNKS), :]
            out_ref[:, s * LANES:(s + 1) * LANES] = chunk.astype(out_ref.dtype)

    @pl.when(i >= nt)
    def _():
        out_ref[...] = jnp.zeros_like(out_ref)


def _dispatch(src, nt, t_rows, n_rows):
    tm = MOE_TM
    grid_spec = pltpu.PrefetchScalarGridSpec(
        num_scalar_prefetch=2,
        grid=(n_rows // tm,),
        in_specs=[pl.BlockSpec(memory_space=pl.ANY)],
        out_specs=pl.BlockSpec((tm, D_MODEL), lambda i, src, nt: (i, 0)),
        scratch_shapes=[pltpu.VMEM((2, tm * ROW_CHUNKS, LANES), F32),
                        pltpu.SemaphoreType.DMA((2,))],
    )
    return pl.pallas_call(
        _dispatch_kernel,
        grid_spec=grid_spec,
        out_shape=jax.ShapeDtypeStruct((n_rows, D_MODEL), BF16),
        compiler_params=_params("arbitrary"),
        name="dispatch",
    )(src, nt, t_rows)


def _gmm1_kernel(te_ref, nx_ref, nt_ref, x_ref, w_hbm, bg_ref, bl_ref, act_ref,
                 wf_buf, wg_scr, wl_scr, sem, *, nj):
    j = pl.program_id(0)
    i = pl.program_id(1)
    tj = act_ref.shape[1]

    def copies(e, jj):
        return [pltpu.make_async_copy(w_hbm.at[e, :, (h * nj + jj) * tj:(h * nj + jj + 1) * tj],
                                      wf_buf.at[h], sem.at[h]) for h in range(2)]

    def start(e, jdyn):
        for jj in range(nj):
            @pl.when(jdyn == jj)
            def _(jj=jj):
                for cp in copies(e, jj):
                    cp.start()

    @pl.when(i < nt_ref[0])
    def _():
        e = te_ref[i]
        new_group = (i == 0) | (e != te_ref[jnp.maximum(i - 1, 0)])

        @pl.when((i == 0) & (j == 0))
        def _():
            start(e, j)

        @pl.when(new_group)
        def _():
            for cp in copies(e, 0):
                cp.wait()
            wg_scr[...] = wf_buf[0].astype(BF16)
            wl_scr[...] = wf_buf[1].astype(BF16)
            nxt = nx_ref[i]

            @pl.when(nxt >= 0)
            def _():
                start(nxt, j)

            @pl.when((nxt < 0) & (j + 1 < nj))
            def _():
                start(te_ref[0], j + 1)

        x = x_ref[...]
        glu = jnp.dot(x, wg_scr[...], preferred_element_type=F32) + bg_ref[...]
        lin = jnp.dot(x, wl_scr[...], preferred_element_type=F32) + bl_ref[...]
        glu = jnp.minimum(glu, SWIGLU_LIMIT)
        lin = jnp.clip(lin, -SWIGLU_LIMIT, SWIGLU_LIMIT)
        act_ref[...] = (glu * _sigmoid(SWIGLU_ALPHA * glu) * (lin + 1.0)).astype(act_ref.dtype)

    @pl.when(i >= nt_ref[0])
    def _():
        act_ref[...] = jnp.zeros_like(act_ref)


def _gmm1(te, nx, nt, xs, w_up, b_up3):
    n_rows, d = xs.shape
    de = w_up.shape[2] // 2
    tm, tj = MOE_TM, 1024
    nj = de // tj
    row = lambda j, i, te, nx, nt: jnp.maximum(jnp.minimum(i, nt[0] - 1), 0)
    exp = lambda j, i, te, nx, nt: te[row(j, i, te, nx, nt)]
    grid_spec = pltpu.PrefetchScalarGridSpec(
        num_scalar_prefetch=3,
        grid=(nj, n_rows // tm),
        in_specs=[pl.BlockSpec((tm, d), lambda j, i, te, nx, nt: (row(j, i, te, nx, nt), 0)),
                  pl.BlockSpec(memory_space=pl.ANY),
                  pl.BlockSpec((None, 1, tj), lambda j, i, te, nx, nt: (exp(j, i, te, nx, nt), 0, j)),
                  pl.BlockSpec((None, 1, tj), lambda j, i, te, nx, nt: (exp(j, i, te, nx, nt), 0, nj + j))],
        out_specs=pl.BlockSpec((tm, tj), lambda j, i, te, nx, nt: (i, j)),
        scratch_shapes=[pltpu.VMEM((2, d, tj), F32), pltpu.VMEM((d, tj), BF16), pltpu.VMEM((d, tj), BF16),
                        pltpu.SemaphoreType.DMA((2,))],
    )
    return pl.pallas_call(
        functools.partial(_gmm1_kernel, nj=nj),
        grid_spec=grid_spec,
        out_shape=jax.ShapeDtypeStruct((n_rows, de), BF16),
        compiler_params=_params("arbitrary", "arbitrary"),
        name="gmm1",
    )(te, nx, nt, xs, w_up, b_up3, b_up3)


def _gmm2_kernel(te_ref, nx_ref, nt_ref, a_ref, wd_hbm, bd_ref, y_ref, wf_buf, wd_scr, sem):
    i = pl.program_id(0)

    def fetch(e):
        return pltpu.make_async_copy(wd_hbm.at[e], wf_buf, sem.at[0])

    @pl.when(i < nt_ref[0])
    def _():
        e = te_ref[i]
        new_group = (i == 0) | (e != te_ref[jnp.maximum(i - 1, 0)])

        @pl.when(i == 0)
        def _():
            fetch(e).start()

        @pl.when(new_group)
        def _():
            fetch(e).wait()
            wd_scr[...] = wf_buf[...].astype(BF16)
            nxt = nx_ref[i]

            @pl.when(nxt >= 0)
            def _():
                fetch(nxt).start()

        y = jnp.dot(a_ref[...], wd_scr[...], preferred_element_type=F32) + bd_ref[...]
        tm = y.shape[0]
        for s in range(ROW_CHUNKS):
            y_ref[pl.ds(s, tm, stride=ROW_CHUNKS), :] = y[:, s * LANES:(s + 1) * LANES]

    @pl.when(i >= nt_ref[0])
    def _():
        y_ref[...] = jnp.zeros_like(y_ref)


def _gmm2(te, nx, nt, act, w_down, b_down3):
    n_rows, de = act.shape
    d = w_down.shape[2]
    tm = MOE_TM
    row = lambda i, te, nx, nt: jnp.maximum(jnp.minimum(i, nt[0] - 1), 0)
    exp = lambda i, te, nx, nt: te[row(i, te, nx, nt)]
    grid_spec = pltpu.PrefetchScalarGridSpec(
        num_scalar_prefetch=3,
        grid=(n_rows // tm,),
        in_specs=[pl.BlockSpec((tm, de), lambda i, te, nx, nt: (row(i, te, nx, nt), 0)),
                  pl.BlockSpec(memory_space=pl.ANY),
                  pl.BlockSpec((None, 1, d), lambda i, te, nx, nt: (exp(i, te, nx, nt), 0, 0))],
        out_specs=pl.BlockSpec((tm * ROW_CHUNKS, LANES), lambda i, te, nx, nt: (i, 0)),
        scratch_shapes=[pltpu.VMEM((de, d), F32), pltpu.VMEM((de, d), BF16),
                        pltpu.SemaphoreType.DMA((1,))],
    )
    return pl.pallas_call(
        _gmm2_kernel,
        grid_spec=grid_spec,
        out_shape=jax.ShapeDtypeStruct((n_rows * ROW_CHUNKS, LANES), F32),
        compiler_params=_params("arbitrary"),
        name="gmm2",
    )(te, nx, nt, act, w_down, b_down3)


def _combine_kernel(pos_ref, y_hbm, rw_ref, x_ref, mod_ref, out_ref, buf, sem):
    i = pl.program_id(0)
    n = pl.num_programs(0)
    tm = out_ref.shape[0]

    def issue(tile, slot):
        def body(r, c):
            for kk in range(TOP_K):
                p = pos_ref[(tile * tm + r) * TOP_K + kk]
                pltpu.make_async_copy(y_hbm.at[pl.ds(p * ROW_CHUNKS, ROW_CHUNKS), :],
                                      buf.at[slot, kk, pl.ds(r * ROW_CHUNKS, ROW_CHUNKS), :],
                                      sem.at[slot]).start(priority=kk % 2)
            return c
        lax.fori_loop(0, tm, body, 0, unroll=4)

    @pl.when(i == 0)
    def _():
        issue(0, 0)

    @pl.when(i + 1 < n)
    def _():
        issue(i + 1, (i + 1) % 2)

    slot = i % 2
    for kk in range(TOP_K):
        pltpu.make_async_copy(y_hbm.at[pl.ds(0, tm * ROW_CHUNKS), :], buf.at[slot, kk],
                              sem.at[slot]).wait()
    rw = rw_ref[...]
    gate = mod_ref[5:6, :]
    for s in range(ROW_CHUNKS):
        acc = None
        for kk in range(TOP_K):
            chunk = buf[slot, kk, pl.ds(s, tm, stride=ROW_CHUNKS), :]
            term = rw[:, kk:kk + 1] * chunk
            acc = term if acc is None else acc + term
        sl = slice(s * LANES, (s + 1) * LANES)
        out_ref[:, sl] = x_ref[:, sl] + gate[:, sl] * acc


def _combine(pos_flat, y_rows, rw, x1, mod3, seq):
    t, d = x1.shape
    tm = 256
    per_b = seq // tm
    grid_spec = pltpu.PrefetchScalarGridSpec(
        num_scalar_prefetch=1,
        grid=(t // tm,),
        in_specs=[pl.BlockSpec(memory_space=pl.ANY),
                  pl.BlockSpec((tm, LANES), lambda i, pos: (i, 0)),
                  pl.BlockSpec((tm, d), lambda i, pos: (i, 0)),
                  pl.BlockSpec((None, 8, d), lambda i, pos: (i // per_b, 0, 0))],
        out_specs=pl.BlockSpec((tm, d), lambda i, pos: (i, 0)),
        scratch_shapes=[pltpu.VMEM((2, TOP_K, tm * ROW_CHUNKS, LANES), F32),
                        pltpu.SemaphoreType.DMA((2,))],
    )
    return pl.pallas_call(
        _combine_kernel,
        grid_spec=grid_spec,
        out_shape=jax.ShapeDtypeStruct((t, d), F32),
        compiler_params=_params("arbitrary"),
        name="combine",
    )(pos_flat, y_rows, rw, x1, mod3)


def _moe(x1, mod3, g_ffn, w_router, b_router, w_up, b_up, w_down, b_down, seq):
    t, d = x1.shape
    ne = w_up.shape[0]
    tm = MOE_TM
    n_rows = t * TOP_K + ne * tm
    n_tiles = n_rows // tm

    wr = jnp.pad(w_router, ((0, 0), (0, LANES - ne)))
    br = jnp.concatenate([b_router, jnp.full((LANES - ne,), NEG_INF, F32)]).reshape(1, LANES)
    t_rows, ri, rw, cnt = _route(x1, mod3, g_ffn.reshape(1, d), wr, br, seq)

    counts = cnt[0, :ne].astype(I32)
    tiles_per = (counts + tm - 1) // tm
    tile_end = jnp.cumsum(tiles_per)
    offs = (tile_end - tiles_per) * tm
    eidx = ri[:, :TOP_K]
    rank = ri[:, TOP_K:2 * TOP_K]
    onehot = eidx[..., None] == jnp.arange(ne, dtype=I32)
    pos = (jnp.sum(jnp.where(onehot, offs, 0), axis=-1) + rank).reshape(-1)
    nt = tile_end[-1:].astype(I32)
    te = jnp.minimum(jnp.sum(jnp.arange(n_tiles, dtype=I32)[:, None] >= tile_end[None, :], axis=1),
                     ne - 1).astype(I32)

    next_tile = jnp.sum(jnp.where(te[:, None] == jnp.arange(ne, dtype=I32), tile_end[None, :], 0), axis=1)
    nx = jnp.where(next_tile < nt[0], te[jnp.minimum(next_tile, n_tiles - 1)], -1).astype(I32)

    src = _inverse(pos, n_rows)
    xs = _dispatch(src, nt, t_rows, n_rows)
    act = _gmm1(te, nx, nt, xs, w_up, b_up.reshape(ne, 1, -1))
    y_rows = _gmm2(te, nx, nt, act, w_down, b_down.reshape(ne, 1, d))
    return _combine(pos, y_rows, rw, x1, mod3, seq)


def _layer(x2, c, batch, seq, w_ada, b_ada, g_mix, w_in, b_igate, b_fgate, g_mlstm_out, g_q, g_k,
           w_branch_a, w_branch_b, w_out, g_ffn, w_router, b_router, w_up, b_up, w_down, b_down):
    d = D_MODEL
    c_pad = jnp.zeros((8, d), F32).at[:batch].set(c)
    mod = _ada(c_pad, w_ada, b_ada)
    mod3 = jnp.pad(mod[:batch].reshape(batch, 6, d), ((0, 0), (0, 2), (0, 0)))

    i0 = Z_QB
    w_main = jnp.concatenate([w_in[:, :i0], w_in[:, i0 + 2 * MLSTM_HEADS:]], axis=1).astype(BF16)
    w_gate = jnp.pad(w_in[:, i0:i0 + 2 * MLSTM_HEADS],
                     ((0, 0), (0, LANES - 2 * MLSTM_HEADS))).astype(BF16)
    gbias = jnp.pad(jnp.concatenate([b_igate, b_fgate]), (0, LANES - 2 * MLSTM_HEADS)).reshape(1, LANES)

    z, gates = _in_proj(x2, mod3, g_mix.reshape(1, d), w_main, w_gate, seq)
    hm = _mlstm(z, gates, gbias, g_mlstm_out, batch, seq)
    ob = _moba(z, g_q.reshape(1, -1), g_k.reshape(1, -1), batch, seq)
    x1 = _mix_out(hm, ob, z, w_branch_a.astype(BF16), w_branch_b.astype(BF16), w_out.astype(BF16),
                  x2, mod3, seq)
    return _moe(x1, mod3, g_ffn, w_router, b_router, w_up, b_up, w_down, b_down, seq)


def kernel(x, c, w_ada, b_ada, g_mix, w_in, b_igate, b_fgate, g_mlstm_out, g_q, g_k, w_branch_a,
           w_branch_b, w_out, g_ffn, w_router, b_router, w_up, b_up, w_down, b_down):
    batch, seq, d = x.shape
    x2 = x.reshape(batch * seq, d)
    for l in range(w_ada.shape[0]):
        x2 = _layer(x2, c, batch, seq, w_ada[l], b_ada[l], g_mix[l], w_in[l], b_igate[l], b_fgate[l],
                    g_mlstm_out[l], g_q[l], g_k[l], w_branch_a[l], w_branch_b[l], w_out[l], g_ffn[l],
                    w_router[l], b_router[l], w_up[l], b_up[l], w_down[l], b_down[l])
    return x2.reshape(batch, seq, d)
```
